```python
import numpy as np
import jax
import jax.numpy as jnp
from jax import lax

D_MODEL = 4096
BATCH = 4
SEQ = 4096
DEPTH = 2

GRID_W = 64
N_BRANCH = 3
BRANCH_WIDTH = D_MODEL // 2
SHORT_CONV_K = 3
NA_HEAD_DIM = 128
NA_HEADS = BRANCH_WIDTH // NA_HEAD_DIM
NA_WIDTH = NA_HEADS * NA_HEAD_DIM
NA_WIN_ROWS = 8
NA_WIN_COLS = 16
NA_QBLOCK_COLS = 16
GLA_HEADS = 4
GLA_VALUE_WIDTH = BRANCH_WIDTH
GLA_KEY_WIDTH = BRANCH_WIDTH // 2
GLA_KEY_DIM = GLA_KEY_WIDTH // GLA_HEADS
GLA_VALUE_DIM = GLA_VALUE_WIDTH // GLA_HEADS
GLA_GATE_RANK = 16
GLA_GATE_TAU = 16.0
GLA_CHUNK = 64
D_FF = 256 * ((8 * D_MODEL // 3 + 255) // 256)
FFN_CONV_K = 3
LN_EPS = 1e-5
RMS_EPS = 1e-6
NEG_INF = -1e30
DEEPNORM_ALPHA = (2.0 * DEPTH) ** 0.25
DEEPNORM_BETA = (8.0 * DEPTH) ** -0.25
IN_SPLIT_SIZES = (
    BRANCH_WIDTH, BRANCH_WIDTH, BRANCH_WIDTH,
    NA_WIDTH, NA_WIDTH, NA_WIDTH,
    GLA_KEY_WIDTH, GLA_KEY_WIDTH,
    GLA_VALUE_WIDTH, GLA_VALUE_WIDTH,
    2 * GLA_GATE_RANK,
    N_BRANCH * D_MODEL,
)
IN_SPLIT_POINTS = tuple(int(p) for p in np.cumsum(IN_SPLIT_SIZES)[:-1])
IN_WIDTH = int(sum(IN_SPLIT_SIZES))
IN_VALUE_SEGMENTS = (2, 5, 8)

kernel_name = "hybrid_conv_natten_gla_encoder"


def layer_norm(x, g, b):
    xf = x.astype(jnp.float32)
    mu = jnp.mean(xf, axis=-1, keepdims=True)
    xc = xf - mu
    var = jnp.mean(xc * xc, axis=-1, keepdims=True)
    return (xc * lax.rsqrt(var + LN_EPS) * g.astype(jnp.float32) + b.astype(jnp.float32)).astype(x.dtype)


def dwconv_centred(x, w):
    k = w.shape[0]
    pad = k // 2
    s = x.shape[1]
    xp = jnp.pad(x, ((0, 0), (pad, pad), (0, 0)))
    return sum(xp[:, i:i + s] * w[i] for i in range(k))


def short_conv_mixer(gate_b, gate_c, h, w_conv):
    return gate_b * dwconv_centred(gate_c * h, w_conv)


def _band_indices(length, win, qblock):
    kblock = min(qblock + win - 1, length)
    nblk = length // qblock
    q_pos = np.arange(nblk)[:, None] * qblock + np.arange(qblock)[None, :]
    k_start = np.clip(np.arange(nblk) * qblock - win // 2, 0, length - kblock)
    k_pos = k_start[:, None] + np.arange(kblock)[None, :]
    w_start = np.clip(q_pos - win // 2, 0, length - win)
    rel = k_pos[:, None, :] - q_pos[:, :, None]
    valid = (k_pos[:, None, :] >= w_start[..., None]) & (k_pos[:, None, :] < w_start[..., None] + win)
    return k_pos, rel, valid


def neighbourhood_attention(q, k, v, rpb):
    bsz, s, nh, dh = q.shape
    rows = s // GRID_W
    win_r = min(NA_WIN_ROWS, rows)
    qb_r = max(d for d in range(1, win_r + 1) if rows % d == 0)
    rk, rrel, rvalid = _band_indices(rows, win_r, qb_r)
    ck, crel, cvalid = _band_indices(GRID_W, NA_WIN_COLS, NA_QBLOCK_COLS)
    nr, qr, kr = rrel.shape
    nc, qc, kc = crel.shape
    qg = q.reshape(bsz, nr, qr, nc, qc, nh, dh)
    ridx = rk[:, :, None, None]
    cidx = ck[None, None, :, :]
    kb = k.reshape(bsz, rows, GRID_W, nh, dh)[:, ridx, cidx]
    vb = v.reshape(bsz, rows, GRID_W, nh, dh)[:, ridx, cidx]
    scores = jnp.einsum('bnqmphd,bnkmjhd->bhnmqpkj', qg, kb,
                        preferred_element_type=jnp.float32) * (dh ** -0.5)
    ri = np.clip(rrel + NA_WIN_ROWS - 1, 0, 2 * NA_WIN_ROWS - 2)[:, None, :, None, :, None]
    ci = np.clip(crel + NA_WIN_COLS - 1, 0, 2 * NA_WIN_COLS - 2)[None, :, None, :, None, :]
    bias = rpb[:, ri, ci].astype(jnp.float32)
    mask = rvalid[:, None, :, None, :, None] & cvalid[None, :, None, :, None, :]
    scores = jnp.where(mask, scores + bias[None], NEG_INF)
    probs = jax.nn.softmax(scores, axis=(-2, -1)).astype(v.dtype)
    out = jnp.einsum('bhnmqpkj,bnkmjhd->bnqmphd', probs, vb)
    return out.reshape(bsz, s, nh * dh)


def gla_chunked(q, k, v, log_a):
    bsz, s, nh, dk = q.shape
    dv = v.shape[-1]
    n = s // GLA_CHUNK
    q, k, log_a = (t.reshape(bsz, n, GLA_CHUNK, nh, dk) for t in (q, k, log_a))
    v = v.reshape(bsz, n, GLA_CHUNK, nh, dv)
    cum = jnp.cumsum(log_a, axis=2)
    last = cum[:, :, -1:]
    q_in = q * jnp.exp(cum)
    k_in = k * jnp.exp(-cum)
    k_out = k * jnp.exp(last - cum)
    tri = np.tril(np.ones((GLA_CHUNK, GLA_CHUNK), dtype=bool))
    att = jnp.where(tri, jnp.einsum('bnihd,bnjhd->bnhij', q_in, k_in), 0.0)
    o_intra = jnp.einsum('bnhij,bnjhe->bnihe', att, v)

    def step(state, xs):
        qc, kc, vc, dec = xs
        o = jnp.einsum('bihd,bhde->bihe', qc, state)
        state = state * dec[..., None] + jnp.einsum('bjhd,bjhe->bhde', kc, vc)
        return state, o

    xs = (jnp.moveaxis(q_in, 1, 0), jnp.moveaxis(k_out, 1, 0), jnp.moveaxis(v, 1, 0),
          jnp.moveaxis(jnp.exp(last[:, :, 0]), 1, 0))
    _, o_inter = lax.scan(step, jnp.zeros((bsz, nh, dk, dv), jnp.float32), xs)
    return (o_intra + jnp.moveaxis(o_inter, 0, 1)).reshape(bsz, s, nh, dv)


def gla_mixer(q, k, v, r, a_lr, w_decay, b_decay, norm_g):
    bsz, s, _ = q.shape
    f32 = jnp.float32
    qh = q.astype(f32).reshape(bsz, s, GLA_HEADS, GLA_KEY_DIM) * (GLA_KEY_DIM ** -0.5)
    kh = k.astype(f32).reshape(bsz, s, GLA_HEADS, GLA_KEY_DIM)
    vh = v.astype(f32).reshape(bsz, s, GLA_HEADS, GLA_VALUE_DIM)
    z = jnp.einsum('bsgr,grk->bsgk', a_lr.reshape(bsz, s, 2, GLA_GATE_RANK), w_decay) + b_decay
    log_a = (jax.nn.log_sigmoid(z.astype(f32)) / GLA_GATE_TAU).reshape(bsz, s, 2, GLA_HEADS, GLA_KEY_DIM)
    o_fwd = gla_chunked(qh, kh, vh, log_a[:, :, 0])
    o_bwd = jnp.flip(gla_chunked(jnp.flip(qh, 1), jnp.flip(kh, 1), jnp.flip(vh, 1),
                                 jnp.flip(log_a[:, :, 1], 1)), 1)
    o = o_fwd + o_bwd
    o = o * lax.rsqrt(jnp.mean(o * o, axis=-1, keepdims=True) + RMS_EPS)
    o = o.reshape(bsz, s, GLA_VALUE_WIDTH) * norm_g.astype(f32)
    return (o * jax.nn.silu(r.astype(f32))).astype(q.dtype)


def conv_glu_ffn(x, w_in, w_conv, b_conv, w_out):
    a, u = jnp.split(x @ w_in, 2, axis=-1)
    a = dwconv_centred(a, w_conv) + b_conv
    return (jax.nn.gelu(a, approximate=False) * u) @ w_out


def setup_inputs(seed: int = 0) -> dict:
    key = jax.random.key(seed)
    ks = jax.random.split(key, 20)
    d = D_MODEL
    nrm = jax.random.normal
    col_scale = np.ones((IN_WIDTH,), dtype=np.float32)
    starts = (0,) + IN_SPLIT_POINTS
    for seg in IN_VALUE_SEGMENTS:
        col_scale[starts[seg]:starts[seg] + IN_SPLIT_SIZES[seg]] = DEEPNORM_BETA
    ffn_scale = np.concatenate([np.ones((D_FF,), np.float32),
                                np.full((D_FF,), DEEPNORM_BETA, np.float32)])
    return {
        "x": nrm(ks[0], (BATCH, SEQ, d), jnp.float32),
        "w_in": nrm(ks[1], (DEPTH, d, IN_WIDTH), jnp.float32) * (col_scale * d ** -0.5),
        "b_gate": 0.1 * nrm(ks[2], (DEPTH, N_BRANCH * d), jnp.float32),
        "w_short_conv": nrm(ks[3], (DEPTH, SHORT_CONV_K, BRANCH_WIDTH), jnp.float32) * SHORT_CONV_K ** -0.5,
        "na_rpb": 0.1 * nrm(ks[4], (DEPTH, NA_HEADS, 2 * NA_WIN_ROWS - 1, 2 * NA_WIN_COLS - 1), jnp.float32),
        "gla_w_decay": nrm(ks[5], (DEPTH, 2, GLA_GATE_RANK, GLA_KEY_WIDTH), jnp.float32) * GLA_GATE_RANK ** -0.5,
        "gla_b_decay": 0.1 * nrm(ks[6], (DEPTH, 2, GLA_KEY_WIDTH), jnp.float32),
        "gla_norm_g": 1.0 + 0.02 * nrm(ks[7], (DEPTH, GLA_VALUE_WIDTH), jnp.float32),
        "w_branch": nrm(ks[8], (DEPTH, N_BRANCH, BRANCH_WIDTH, d), jnp.float32) * BRANCH_WIDTH ** -0.5,
        "w_out": nrm(ks[9], (DEPTH, d, d), jnp.float32) * (DEEPNORM_BETA * d ** -0.5),
        "ln1_g": 1.0 + 0.02 * nrm(ks[10], (DEPTH, d), jnp.float32),
        "ln1_b": 0.02 * nrm(ks[11], (DEPTH, d), jnp.float32),
        "w_ffn_in": nrm(ks[12], (DEPTH, d, 2 * D_FF), jnp.float32) * (ffn_scale * d ** -0.5),
        "w_ffn_conv": nrm(ks[13], (DEPTH, FFN_CONV_K, D_FF), jnp.float32) * FFN_CONV_K ** -0.5,
        "b_ffn_conv": 0.02 * nrm(ks[14], (DEPTH, D_FF), jnp.float32),
        "w_ffn_out": nrm(ks[15], (DEPTH, D_FF, d), jnp.float32) * (DEEPNORM_BETA * D_FF ** -0.5),
        "ln2_g": 1.0 + 0.02 * nrm(ks[16], (DEPTH, d), jnp.float32),
        "ln2_b": 0.02 * nrm(ks[17], (DEPTH, d), jnp.float32),
    }


def reference(x, w_in, b_gate, w_short_conv, na_rpb, gla_w_decay, gla_b_decay, gla_norm_g,
              w_branch, w_out, ln1_g, ln1_b, w_ffn_in, w_ffn_conv, b_ffn_conv, w_ffn_out,
              ln2_g, ln2_b):
    bsz, s, d = x.shape
    for l in range(DEPTH):
        h = x @ w_in[l]
        (c_b, c_c, c_h, na_q, na_k, na_v, g_q, g_k, g_v, g_r, g_a, gate_lin) = jnp.split(
            h, IN_SPLIT_POINTS, axis=-1)
        y_conv = short_conv_mixer(c_b, c_c, c_h, w_short_conv[l])
        y_na = neighbourhood_attention(
            na_q.reshape(bsz, s, NA_HEADS, NA_HEAD_DIM),
            na_k.reshape(bsz, s, NA_HEADS, NA_HEAD_DIM),
            na_v.reshape(bsz, s, NA_HEADS, NA_HEAD_DIM), na_rpb[l])
        y_gla = gla_mixer(g_q, g_k, g_v, g_r, g_a, gla_w_decay[l], gla_b_decay[l], gla_norm_g[l])
        gates = jax.nn.sigmoid((gate_lin + b_gate[l]).reshape(bsz, s, N_BRANCH, d))
        mixed = (gates[:, :, 0] * (y_conv @ w_branch[l, 0])
                 + gates[:, :, 1] * (y_na @ w_branch[l, 1])
                 + gates[:, :, 2] * (y_gla @ w_branch[l, 2]))
        x = layer_norm(DEEPNORM_ALPHA * x + mixed @ w_out[l], ln1_g[l], ln1_b[l])
        f = conv_glu_ffn(x, w_ffn_in[l], w_ffn_conv[l], b_ffn_conv[l], w_ffn_out[l])
        x = layer_norm(DEEPNORM_ALPHA * x + f, ln2_g[l], ln2_b[l])
    return x
```

```python
import functools

import numpy as np
import jax
import jax.numpy as jnp
from jax import lax
from jax.experimental import pallas as pl
from jax.experimental.pallas import tpu as pltpu

F32 = jnp.float32
BF16 = jnp.bfloat16

D_MODEL = 4096
SEQ = 4096
GRID_W = 64
ROWS = SEQ // GRID_W
BRANCH_WIDTH = D_MODEL // 2
N_BRANCH = 3
NA_HEAD_DIM = 128
NA_HEADS = BRANCH_WIDTH // NA_HEAD_DIM
NA_WIN_ROWS = 8
NA_WIN_COLS = 16
NA_QROWS = 8
NA_KROWS = 16
GLA_HEADS = 4
GLA_KEY_WIDTH = BRANCH_WIDTH // 2
GLA_KEY_DIM = GLA_KEY_WIDTH // GLA_HEADS
GLA_VALUE_DIM = BRANCH_WIDTH // GLA_HEADS
GLA_GATE_RANK = 16
GLA_GATE_TAU = 16.0
GLA_CHUNK = 64
N_CHUNKS = SEQ // GLA_CHUNK
D_FF = 256 * ((8 * D_MODEL // 3 + 255) // 256)
LN_EPS = 1e-5
RMS_EPS = 1e-6
NEG_INF = -1e30

OFF_CONV_B = 0
OFF_CONV_C = BRANCH_WIDTH
OFF_CONV_H = 2 * BRANCH_WIDTH
OFF_NA_Q = 3 * BRANCH_WIDTH
OFF_NA_K = 4 * BRANCH_WIDTH
OFF_NA_V = 5 * BRANCH_WIDTH
OFF_GLA_Q = 6 * BRANCH_WIDTH
OFF_GLA_K = OFF_GLA_Q + GLA_KEY_WIDTH
OFF_GLA_V = OFF_GLA_K + GLA_KEY_WIDTH
OFF_GLA_R = OFF_GLA_V + BRANCH_WIDTH
MAIN_WIDTH = OFF_GLA_R + BRANCH_WIDTH
OFF_GATE = MAIN_WIDTH + 2 * GLA_GATE_RANK

LANES = 128
V7X_VMEM_BYTES = 64 * 1024 * 1024
MM_TILE = 1024
D_FF_PAD = MM_TILE * (-(-D_FF // MM_TILE))
MIB = 1024 * 1024


def _cparams(semantics, vmem_mib):
    assert vmem_mib * MIB < V7X_VMEM_BYTES
    return pltpu.CompilerParams(dimension_semantics=semantics, vmem_limit_bytes=vmem_mib * MIB)


def _mm_kernel(a_ref, b_ref, o_ref):
    o_ref[...] = jnp.dot(a_ref[...], b_ref[...], preferred_element_type=F32).astype(o_ref.dtype)


def _mm_sigmoid_kernel(a_ref, b_ref, bias_ref, o_ref):
    acc = jnp.dot(a_ref[...], b_ref[...], preferred_element_type=F32)
    o_ref[...] = jax.nn.sigmoid(acc + bias_ref[...]).astype(o_ref.dtype)


def matmul(a, b, out_dtype, bias=None, tm=MM_TILE, tn=MM_TILE):
    m, k = a.shape
    _, n = b.shape
    tn = min(tn, n)
    in_specs = [pl.BlockSpec((tm, k), lambda i, j: (i, 0)),
                pl.BlockSpec((k, tn), lambda i, j: (0, j))]
    args = [a, b]
    body = _mm_kernel
    if bias is not None:
        in_specs.append(pl.BlockSpec((1, tn), lambda i, j: (0, j)))
        args.append(bias.reshape(1, n))
        body = _mm_sigmoid_kernel
    return pl.pallas_call(
        body,
        grid=(m // tm, n // tn),
        in_specs=in_specs,
        out_specs=pl.BlockSpec((tm, tn), lambda i, j: (i, j)),
        out_shape=jax.ShapeDtypeStruct((m, n), out_dtype),
        compiler_params=_cparams(("parallel", "parallel"), 48),
        name="mm_sigmoid" if bias is not None else "mm",
    )(*args)


def _mm_kacc_kernel(a_ref, b_ref, o_ref):
    @pl.when(pl.program_id(1) == 0)
    def _():
        o_ref[...] = jnp.zeros_like(o_ref)

    o_ref[...] += jnp.dot(a_ref[...], b_ref[...], preferred_element_type=F32)


def matmul_kacc(a, b, tm=MM_TILE, tk=MM_TILE):
    m, k = a.shape
    _, n = b.shape
    return pl.pallas_call(
        _mm_kacc_kernel,
        grid=(m // tm, k // tk),
        in_specs=[pl.BlockSpec((tm, tk), lambda i, kk: (i, kk)),
                  pl.BlockSpec((tk, n), lambda i, kk: (kk, 0))],
        out_specs=pl.BlockSpec((tm, n), lambda i, kk: (i, 0)),
        out_shape=jax.ShapeDtypeStruct((m, n), F32),
        compiler_params=_cparams(("parallel", "arbitrary"), 56),
        name="mm_kacc",
    )(a, b)


def _conv3(p, w_ref):
    t = lax.broadcasted_iota(jnp.int32, p.shape, 0)
    prev = jnp.where(t == 0, 0.0, pltpu.roll(p, 1, 0))
    nxt = jnp.where(t == SEQ - 1, 0.0, pltpu.roll(p, SEQ - 1, 0))
    return prev * w_ref[0:1, :] + p * w_ref[1:2, :] + nxt * w_ref[2:3, :]


def _short_conv_kernel(b_ref, c_ref, h_ref, w_ref, o_ref):
    p = c_ref[...].astype(F32) * h_ref[...].astype(F32)
    o_ref[...] = (b_ref[...].astype(F32) * _conv3(p, w_ref)).astype(o_ref.dtype)


def short_conv_mixer(hm, w_conv, bsz, tc=256):
    nb = BRANCH_WIDTH // tc

    def spec(off):
        return pl.BlockSpec((SEQ, tc), lambda b, j: (b, off // tc + j))

    return pl.pallas_call(
        _short_conv_kernel,
        grid=(bsz, nb),
        in_specs=[spec(OFF_CONV_B), spec(OFF_CONV_C), spec(OFF_CONV_H),
                  pl.BlockSpec((3, tc), lambda b, j: (0, j))],
        out_specs=pl.BlockSpec((SEQ, tc), lambda b, j: (b, j)),
        out_shape=jax.ShapeDtypeStruct((bsz * SEQ, BRANCH_WIDTH), BF16),
        compiler_params=_cparams(("parallel", "parallel"), 48),
        name="short_conv",
    )(hm, hm, hm, w_conv)


def _ffn_act_kernel(a_ref, u_ref, w_ref, bias_ref, o_ref):
    a = _conv3(a_ref[...].astype(F32), w_ref) + bias_ref[...]
    g = 0.5 * a * (1.0 + lax.erf(a * (2.0 ** -0.5)))
    o_ref[...] = (g * u_ref[...].astype(F32)).astype(o_ref.dtype)


def ffn_act(hf, w_conv, b_conv, bsz, tc=256):
    nb = D_FF_PAD // tc
    return pl.pallas_call(
        _ffn_act_kernel,
        grid=(bsz, nb),
        in_specs=[pl.BlockSpec((SEQ, tc), lambda b, j: (b, j)),
                  pl.BlockSpec((SEQ, tc), lambda b, j: (b, nb + j)),
                  pl.BlockSpec((3, tc), lambda b, j: (0, j)),
                  pl.BlockSpec((1, tc), lambda b, j: (0, j))],
        out_specs=pl.BlockSpec((SEQ, tc), lambda b, j: (b, j)),
        out_shape=jax.ShapeDtypeStruct((bsz * SEQ, D_FF_PAD), BF16),
        compiler_params=_cparams(("parallel", "parallel"), 48),
        name="ffn_act",
    )(hf, hf, w_conv, b_conv)


def _na_key_start(band):
    lo = band * NA_QROWS - NA_WIN_ROWS // 2
    hi = ROWS - NA_KROWS
    if isinstance(band, int):
        return min(max(lo, 0), hi)
    return jnp.clip(lo, 0, hi)


def _na_bias_tables(rpb):
    qc = np.arange(GRID_W)[:, None]
    kc = np.arange(GRID_W)[None, :]
    wc = np.clip(qc - NA_WIN_COLS // 2, 0, GRID_W - NA_WIN_COLS)
    col_valid = (kc >= wc) & (kc < wc + NA_WIN_COLS)
    cidx = np.clip(kc - qc + NA_WIN_COLS - 1, 0, 2 * NA_WIN_COLS - 2)
    tm = jnp.where(col_valid[None, None], rpb[:, :, cidx], NEG_INF)
    tm = jnp.concatenate([tm, jnp.full_like(tm[:, :1], NEG_INF)], axis=1)
    nbands = ROWS // NA_QROWS
    blk = np.zeros((3, NA_QROWS, NA_KROWS), np.int32)
    for c, band in enumerate((0, 1, nbands - 1)):
        k0 = _na_key_start(band)
        for i in range(NA_QROWS):
            r = band * NA_QROWS + i
            wr = min(max(r - NA_WIN_ROWS // 2, 0), ROWS - NA_WIN_ROWS)
            for j in range(NA_KROWS):
                kr = k0 + j
                blk[c, i, j] = kr - r + NA_WIN_ROWS - 1 if wr <= kr < wr + NA_WIN_ROWS else 2 * NA_WIN_ROWS - 1
    t = tm[:, blk]
    t = jnp.transpose(t, (0, 1, 2, 4, 3, 5))
    return t.reshape(rpb.shape[0], 3, NA_QROWS * GRID_W, NA_KROWS * GRID_W)


def _na_kernel(q_ref, k_ref, v_ref, bias_ref, o_ref):
    band = pl.program_id(2)
    k0 = pl.multiple_of(_na_key_start(band) * GRID_W, GRID_W)
    nk = NA_KROWS * GRID_W
    k = k_ref[pl.ds(k0, nk), :]
    v = v_ref[pl.ds(k0, nk), :]
    s = lax.dot_general(q_ref[...], k, (((1,), (1,)), ((), ())), preferred_element_type=F32)
    s = s * (NA_HEAD_DIM ** -0.5) + bias_ref[0, 0]
    m = jnp.max(s, axis=-1, keepdims=True)
    p = jnp.exp(s - m)
    l = jnp.sum(p, axis=-1, keepdims=True)
    o = jnp.dot(p.astype(BF16), v, preferred_element_type=F32)
    o_ref[...] = (o / l).astype(o_ref.dtype)


def neighbourhood_attention(hm, bias_tab, bsz):
    nbands = ROWS // NA_QROWS
    tq = NA_QROWS * GRID_W
    qo, ko, vo = (off // NA_HEAD_DIM for off in (OFF_NA_Q, OFF_NA_K, OFF_NA_V))
    return pl.pallas_call(
        _na_kernel,
        grid=(NA_HEADS, bsz, nbands),
        in_specs=[
            pl.BlockSpec((tq, NA_HEAD_DIM), lambda h, b, r: (b * nbands + r, qo + h)),
            pl.BlockSpec((SEQ, NA_HEAD_DIM), lambda h, b, r: (b, ko + h)),
            pl.BlockSpec((SEQ, NA_HEAD_DIM), lambda h, b, r: (b, vo + h)),
            pl.BlockSpec((1, 1, tq, NA_KROWS * GRID_W),
                         lambda h, b, r: (h, jnp.minimum(r, 1) + r // (nbands - 1), 0, 0)),
        ],
        out_specs=pl.BlockSpec((tq, NA_HEAD_DIM), lambda h, b, r: (b * nbands + r, h)),
        out_shape=jax.ShapeDtypeStruct((bsz * SEQ, BRANCH_WIDTH), BF16),
        compiler_params=_cparams(("parallel", "parallel", "parallel"), 32),
        name="natten",
    )(hm, hm, hm, bias_tab)


def _gla_scan_kernel(qf_ref, kf_ref, vf_ref, af_ref, qb_ref, kb_ref, vb_ref, ab_ref,
                     wdec_ref, bdec_ref, of_ref, ob_ref, state_ref):
    @pl.when(pl.program_id(1) == 0)
    def _():
        state_ref[...] = jnp.zeros_like(state_ref)

    row = lax.broadcasted_iota(jnp.int32, (GLA_CHUNK, GLA_CHUNK), 0)
    col = lax.broadcasted_iota(jnp.int32, (GLA_CHUNK, GLA_CHUNK), 1)
    nt = (((1,), (1,)), ((), ()))
    tn = (((0,), (0,)), ((), ()))
    dirs = ((qf_ref, kf_ref, vf_ref, af_ref, of_ref, row >= col, GLA_CHUNK - 1),
            (qb_ref, kb_ref, vb_ref, ab_ref, ob_ref, row <= col, 0))
    for d, (q_ref, k_ref, v_ref, a_ref, o_ref, tri, last_row) in enumerate(dirs):
        z = jnp.dot(a_ref[...].astype(BF16), wdec_ref[d], preferred_element_type=F32) + bdec_ref[d:d + 1, :]
        log_a = (jnp.minimum(z, 0.0) - jnp.log1p(jnp.exp(-jnp.abs(z)))) * (1.0 / GLA_GATE_TAU)
        cum = jnp.dot(tri.astype(F32), log_a, precision=lax.Precision.HIGHEST, preferred_element_type=F32)
        last = cum[last_row:last_row + 1, :]
        q_in = q_ref[...].astype(F32) * ((GLA_KEY_DIM ** -0.5) * jnp.exp(cum))
        kk = k_ref[...].astype(F32)
        k_in = kk * jnp.exp(-cum)
        k_out = kk * jnp.exp(last - cum)
        dec = jnp.exp(last)
        for h in range(GLA_HEADS):
            ks = slice(h * GLA_KEY_DIM, (h + 1) * GLA_KEY_DIM)
            vs = slice(h * GLA_VALUE_DIM, (h + 1) * GLA_VALUE_DIM)
            qh = q_in[:, ks].astype(BF16)
            vh = v_ref[:, vs]
            att = lax.dot_general(qh, k_in[:, ks].astype(BF16), nt, preferred_element_type=F32)
            att = jnp.where(tri, att, 0.0).astype(BF16)
            st = state_ref[d, h]
            o = jnp.dot(att, vh, preferred_element_type=F32)
            o = o + lax.dot_general(qh, st.astype(BF16), nt, preferred_element_type=F32)
            o_ref[:, vs] = o
            upd = lax.dot_general(vh, k_out[:, ks].astype(BF16), tn, preferred_element_type=F32)
            state_ref[d, h] = st * dec[:, ks] + upd


def gla_scan(hm, a_lr, wdec, bdec, bsz):
    qo = OFF_GLA_Q // GLA_KEY_WIDTH
    ko = OFF_GLA_K // GLA_KEY_WIDTH
    vo = OFF_GLA_V // BRANCH_WIDTH

    def fwd(colblk):
        return lambda b, c: (b * N_CHUNKS + c, colblk)

    def bwd(colblk):
        return lambda b, c: (b * N_CHUNKS + N_CHUNKS - 1 - c, colblk)

    def specs(idx):
        return [pl.BlockSpec((GLA_CHUNK, GLA_KEY_WIDTH), idx(qo)),
                pl.BlockSpec((GLA_CHUNK, GLA_KEY_WIDTH), idx(ko)),
                pl.BlockSpec((GLA_CHUNK, BRANCH_WIDTH), idx(vo)),
                pl.BlockSpec((GLA_CHUNK, LANES), idx(0))]

    out = jax.ShapeDtypeStruct((bsz * SEQ, BRANCH_WIDTH), F32)
    return pl.pallas_call(
        _gla_scan_kernel,
        grid=(bsz, N_CHUNKS),
        in_specs=specs(fwd) + specs(bwd) + [
            pl.BlockSpec((2, LANES, GLA_KEY_WIDTH), lambda b, c: (0, 0, 0)),
            pl.BlockSpec((2, GLA_KEY_WIDTH), lambda b, c: (0, 0))],
        out_specs=[pl.BlockSpec((GLA_CHUNK, BRANCH_WIDTH), fwd(0)),
                   pl.BlockSpec((GLA_CHUNK, BRANCH_WIDTH), bwd(0))],
        out_shape=[out, out],
        scratch_shapes=[pltpu.VMEM((2, GLA_HEADS, GLA_VALUE_DIM, GLA_KEY_DIM), F32)],
        compiler_params=_cparams(("parallel", "arbitrary"), 32),
        name="gla_scan",
    )(hm, hm, hm, a_lr, hm, hm, hm, a_lr, wdec, bdec)


def _gla_out_kernel(of_ref, ob_ref, r_ref, g_ref, o_ref):
    for h in range(GLA_HEADS):
        vs = slice(h * GLA_VALUE_DIM, (h + 1) * GLA_VALUE_DIM)
        o = of_ref[:, vs] + ob_ref[:, vs]
        o = o * lax.rsqrt(jnp.mean(o * o, axis=-1, keepdims=True) + RMS_EPS) * g_ref[:, vs]
        r = r_ref[:, vs].astype(F32)
        o_ref[:, vs] = (o * (r * jax.nn.sigmoid(r))).astype(o_ref.dtype)


def gla_output(o_f, o_b, hm, norm_g, tm=512):
    m = o_f.shape[0]
    row = pl.BlockSpec((tm, BRANCH_WIDTH), lambda i: (i, 0))
    return pl.pallas_call(
        _gla_out_kernel,
        grid=(m // tm,),
        in_specs=[row, row,
                  pl.BlockSpec((tm, BRANCH_WIDTH), lambda i: (i, OFF_GLA_R // BRANCH_WIDTH)),
                  pl.BlockSpec((1, BRANCH_WIDTH), lambda i: (0, 0))],
        out_specs=row,
        out_shape=jax.ShapeDtypeStruct((m, BRANCH_WIDTH), BF16),
        compiler_params=_cparams(("parallel",), 40),
        name="gla_output",
    )(o_f, o_b, hm, norm_g.reshape(1, BRANCH_WIDTH))


def _mix_kernel(y0_ref, y1_ref, y2_ref, w_ref, g0_ref, g1_ref, g2_ref, o_ref):
    acc = None
    for i, (y_ref, g_ref) in enumerate(((y0_ref, g0_ref), (y1_ref, g1_ref), (y2_ref, g2_ref))):
        t = g_ref[...].astype(F32) * jnp.dot(y_ref[...], w_ref[i], preferred_element_type=F32)
        acc = t if acc is None else acc + t
    o_ref[...] = acc.astype(o_ref.dtype)


def gated_mix(ys, w_branch, gates, tm=MM_TILE, tn=512):
    m = ys[0].shape[0]
    nj = D_MODEL // tn
    yspec = pl.BlockSpec((tm, BRANCH_WIDTH), lambda i, j: (i, 0))

    def gspec(br):
        return pl.BlockSpec((tm, tn), lambda i, j: (i, br * nj + j))

    return pl.pallas_call(
        _mix_kernel,
        grid=(m // tm, nj),
        in_specs=[yspec, yspec, yspec,
                  pl.BlockSpec((N_BRANCH, BRANCH_WIDTH, tn), lambda i, j: (0, 0, j)),
                  gspec(0), gspec(1), gspec(2)],
        out_specs=pl.BlockSpec((tm, tn), lambda i, j: (i, j)),
        out_shape=jax.ShapeDtypeStruct((m, D_MODEL), BF16),
        compiler_params=_cparams(("parallel", "parallel"), 52),
        name="gated_mix",
    )(*ys, w_branch, gates, gates, gates)


def _ln_kernel(x_ref, s_ref, g_ref, b_ref, of_ref, ob_ref, *, alpha):
    y = alpha * x_ref[...] + s_ref[...]
    mu = jnp.mean(y, axis=-1, keepdims=True)
    yc = y - mu
    var = jnp.mean(yc * yc, axis=-1, keepdims=True)
    out = yc * lax.rsqrt(var + LN_EPS) * g_ref[...] + b_ref[...]
    of_ref[...] = out
    ob_ref[...] = out.astype(BF16)


def residual_layer_norm(x, s, g, b, alpha, tm=256):
    m, d = x.shape
    row = pl.BlockSpec((tm, d), lambda i: (i, 0))
    vec = pl.BlockSpec((1, d), lambda i: (0, 0))
    return pl.pallas_call(
        functools.partial(_ln_kernel, alpha=alpha),
        grid=(m // tm,),
        in_specs=[row, row, vec, vec],
        out_specs=[row, row],
        out_shape=[jax.ShapeDtypeStruct((m, d), F32), jax.ShapeDtypeStruct((m, d), BF16)],
        compiler_params=_cparams(("parallel",), 40),
        name="residual_ln",
    )(x, s, g.reshape(1, d), b.reshape(1, d))


def kernel(x, w_in, b_gate, w_short_conv, na_rpb, gla_w_decay, gla_b_decay, gla_norm_g, w_branch, w_out,
           ln1_g, ln1_b, w_ffn_in, w_ffn_conv, b_ffn_conv, w_ffn_out, ln2_g, ln2_b):
    bsz, s, d = x.shape
    depth = w_in.shape[0]
    assert (s, d) == (SEQ, D_MODEL) and w_in.shape[2] == OFF_GATE + N_BRANCH * D_MODEL
    assert w_ffn_in.shape[2] == 2 * D_FF
    alpha = (2.0 * depth) ** 0.25
    ff_pad = D_FF_PAD - D_FF

    xf = x.reshape(bsz * s, d)
    xb = xf.astype(BF16)
    for l in range(depth):
        w_main = w_in[l, :, :MAIN_WIDTH].astype(BF16)
        w_dec_in = jnp.pad(w_in[l, :, MAIN_WIDTH:OFF_GATE], ((0, 0), (0, LANES - 2 * GLA_GATE_RANK))).astype(BF16)
        w_gate = w_in[l, :, OFF_GATE:].astype(BF16)

        hm = matmul(xb, w_main, BF16)
        a_lr = matmul(xb, w_dec_in, F32)
        gates = matmul(xb, w_gate, BF16, bias=b_gate[l])

        y_conv = short_conv_mixer(hm, w_short_conv[l], bsz)
        y_na = neighbourhood_attention(hm, _na_bias_tables(na_rpb[l]), bsz)
        wdec = jnp.zeros((2, LANES, GLA_KEY_WIDTH), F32)
        wdec = wdec.at[0, :GLA_GATE_RANK].set(gla_w_decay[l, 0])
        wdec = wdec.at[1, GLA_GATE_RANK:2 * GLA_GATE_RANK].set(gla_w_decay[l, 1])
        o_f, o_b = gla_scan(hm, a_lr, wdec.astype(BF16), gla_b_decay[l], bsz)
        y_gla = gla_output(o_f, o_b, hm, gla_norm_g[l])

        mixed = gated_mix((y_conv, y_na, y_gla), w_branch[l].astype(BF16), gates)
        s1 = matmul(mixed, w_out[l].astype(BF16), F32)
        xf, xb = residual_layer_norm(xf, s1, ln1_g[l], ln1_b[l], alpha)

        w_ff = w_ffn_in[l]
        w_ff = jnp.concatenate([jnp.pad(w_ff[:, :D_FF], ((0, 0), (0, ff_pad))),
                                jnp.pad(w_ff[:, D_FF:], ((0, 0), (0, ff_pad)))], axis=1).astype(BF16)
        hf = matmul(xb, w_ff, BF16)
        g = ffn_act(hf, jnp.pad(w_ffn_conv[l], ((0, 0), (0, ff_pad))),
                    jnp.pad(b_ffn_conv[l], (0, ff_pad)).reshape(1, D_FF_PAD), bsz)
        s2 = matmul_kacc(g, jnp.pad(w_ffn_out[l], ((0, ff_pad), (0, 0))).astype(BF16))
        xf, xb = residual_layer_norm(xf, s2, ln2_g[l], ln2_b[l], alpha)
    return xf.reshape(bsz, s, d)
```

```python
import functools

import numpy as np
import jax
import jax.numpy as jnp
from jax import lax
from jax.experimental import pallas as pl
from jax.experimental.pallas import tpu as pltpu

F32 = jnp.float32
BF16 = jnp.bfloat16

D_MODEL = 4096
SEQ = 4096
GRID_W = 64
ROWS = SEQ // GRID_W
BRANCH_WIDTH = D_MODEL // 2
N_BRANCH = 3
NA_HEAD_DIM = 128
NA_HEADS = BRANCH_WIDTH // NA_HEAD_DIM
NA_WIN_ROWS = 8
NA_WIN_COLS = 16
NA_QROWS = 8
NA_KROWS = 16
NA_HEADS_PER_STEP = 4
GLA_HEADS = 4
GLA_KEY_WIDTH = BRANCH_WIDTH // 2
GLA_KEY_DIM = GLA_KEY_WIDTH // GLA_HEADS
GLA_VALUE_DIM = BRANCH_WIDTH // GLA_HEADS
GLA_GATE_RANK = 16
GLA_GATE_TAU = 16.0
GLA_CHUNK = 64
N_CHUNKS = SEQ // GLA_CHUNK
D_FF = 256 * ((8 * D_MODEL // 3 + 255) // 256)
LN_EPS = 1e-5
RMS_EPS = 1e-6
NEG_INF = -1e30

OFF_CONV_B = 0
OFF_CONV_C = BRANCH_WIDTH
OFF_CONV_H = 2 * BRANCH_WIDTH
OFF_NA_Q = 3 * BRANCH_WIDTH
OFF_NA_K = 4 * BRANCH_WIDTH
OFF_NA_V = 5 * BRANCH_WIDTH
OFF_GLA_Q = 6 * BRANCH_WIDTH
OFF_GLA_K = OFF_GLA_Q + GLA_KEY_WIDTH
OFF_GLA_V = OFF_GLA_K + GLA_KEY_WIDTH
OFF_GLA_R = OFF_GLA_V + BRANCH_WIDTH
MAIN_WIDTH = OFF_GLA_R + BRANCH_WIDTH
OFF_GATE = MAIN_WIDTH + 2 * GLA_GATE_RANK

LANES = 128
V7X_VMEM_BYTES = 64 * 1024 * 1024
MM_TILE = 1024
D_FF_PAD = MM_TILE * (-(-D_FF // MM_TILE))
FFN_HALO = 16
MIB = 1024 * 1024


def _cparams(semantics, vmem_mib):
    assert vmem_mib * MIB < V7X_VMEM_BYTES
    return pltpu.CompilerParams(dimension_semantics=semantics, vmem_limit_bytes=vmem_mib * MIB)


PREP_ROWS = 1024


def _cast_kernel(src_ref, o_ref):
    o_ref[...] = src_ref[...].astype(o_ref.dtype)


def _cast_or_zero_kernel(src_ref, o_ref, *, axis, pad_index):
    is_pad = pl.program_id(axis) % (pad_index + 1) == pad_index

    @pl.when(is_pad)
    def _():
        o_ref[...] = jnp.zeros_like(o_ref)

    @pl.when(jnp.logical_not(is_pad))
    def _():
        o_ref[...] = src_ref[...].astype(o_ref.dtype)


def _shift_cast_kernel(src_ref, tail_ref, o_ref, *, shift):
    ngroups = o_ref.shape[1] // LANES
    lane = lax.broadcasted_iota(jnp.int32, (o_ref.shape[0], LANES), 1)
    for g in range(ngroups):
        cur = src_ref[:, g * LANES:(g + 1) * LANES]
        nxt = tail_ref[...] if g == ngroups - 1 else src_ref[:, (g + 1) * LANES:(g + 2) * LANES]
        merged = jnp.where(lane < LANES - shift, pltpu.roll(cur, LANES - shift, 1), pltpu.roll(nxt, LANES - shift, 1))
        o_ref[:, g * LANES:(g + 1) * LANES] = merged.astype(o_ref.dtype)


def _head_cast_kernel(src_ref, o_ref, *, keep):
    lane = lax.broadcasted_iota(jnp.int32, src_ref.shape, 1)
    o_ref[...] = jnp.where(lane < keep, src_ref[...], 0.0).astype(o_ref.dtype)


def prep_w_in(w_in):
    depth, d, _ = w_in.shape
    nr = d // PREP_ROWS
    tn = MM_TILE
    n_gate = N_BRANCH * D_MODEL
    dec_w = 2 * GLA_GATE_RANK
    params = _cparams(("parallel", "parallel", "parallel"), 32)
    blk = (None, PREP_ROWS, tn)
    main = pl.pallas_call(
        _cast_kernel, grid=(depth, nr, MAIN_WIDTH // tn),
        in_specs=[pl.BlockSpec(blk, lambda l, r, j: (l, r, j))],
        out_specs=pl.BlockSpec(blk, lambda l, r, j: (l, r, j)),
        out_shape=jax.ShapeDtypeStruct((depth, d, MAIN_WIDTH), BF16),
        compiler_params=params, name="prep_w_main")(w_in)
    jb, lb = MAIN_WIDTH // tn, tn // LANES
    gates = pl.pallas_call(
        functools.partial(_shift_cast_kernel, shift=dec_w), grid=(depth, nr, n_gate // tn),
        in_specs=[pl.BlockSpec(blk, lambda l, r, j: (l, r, jb + j)),
                  pl.BlockSpec((None, PREP_ROWS, LANES), lambda l, r, j: (l, r, (jb + j + 1) * lb))],
        out_specs=pl.BlockSpec(blk, lambda l, r, j: (l, r, j)),
        out_shape=jax.ShapeDtypeStruct((depth, d, n_gate), BF16),
        compiler_params=params, name="prep_w_gate")(w_in, w_in)
    dec = pl.pallas_call(
        functools.partial(_head_cast_kernel, keep=dec_w), grid=(depth, nr),
        in_specs=[pl.BlockSpec((None, PREP_ROWS, LANES), lambda l, r: (l, r, MAIN_WIDTH // LANES))],
        out_specs=pl.BlockSpec((None, PREP_ROWS, LANES), lambda l, r: (l, r, 0)),
        out_shape=jax.ShapeDtypeStruct((depth, d, LANES), BF16),
        compiler_params=_cparams(("parallel", "parallel"), 32), name="prep_w_dec")(w_in)
    return main, gates, dec


def prep_w_ffn(w_ffn_in, w_ffn_out, tb=256):
    depth, d, _ = w_ffn_in.shape
    live, padded = D_FF // tb, D_FF_PAD // tb
    assert live * tb == D_FF and padded == live + 1
    w_in_b = pl.pallas_call(
        functools.partial(_cast_or_zero_kernel, axis=1, pad_index=live),
        grid=(depth, 2 * padded),
        in_specs=[pl.BlockSpec((None, d, tb),
                               lambda l, j: (l, 0, (j // padded) * live + jnp.minimum(j % padded, live - 1)))],
        out_specs=pl.BlockSpec((None, d, tb), lambda l, j: (l, 0, j)),
        out_shape=jax.ShapeDtypeStruct((depth, d, 2 * D_FF_PAD), BF16),
        compiler_params=_cparams(("parallel", "parallel"), 32), name="prep_w_ffn_in")(w_ffn_in)
    w_out_b = pl.pallas_call(
        functools.partial(_cast_or_zero_kernel, axis=1, pad_index=live),
        grid=(depth, padded),
        in_specs=[pl.BlockSpec((None, tb, d), lambda l, i: (l, jnp.minimum(i, live - 1), 0))],
        out_specs=pl.BlockSpec((None, tb, d), lambda l, i: (l, i, 0)),
        out_shape=jax.ShapeDtypeStruct((depth, D_FF_PAD, d), BF16),
        compiler_params=_cparams(("parallel", "parallel"), 32), name="prep_w_ffn_out")(w_ffn_out)
    return w_in_b, w_out_b


def _mm_kernel(a_ref, b_ref, o_ref):
    o_ref[...] = jnp.dot(a_ref[...], b_ref[...], preferred_element_type=F32).astype(o_ref.dtype)


def _mm_sigmoid_kernel(a_ref, b_ref, bias_ref, o_ref):
    acc = jnp.dot(a_ref[...], b_ref[...], preferred_element_type=F32)
    o_ref[...] = jax.nn.sigmoid(acc + bias_ref[...]).astype(o_ref.dtype)


def matmul(a, w, layer, out_dtype, bias=None, tm=MM_TILE, tn=MM_TILE):
    m, k = a.shape
    n = w.shape[2]
    tn = min(tn, n)
    assert n % tn == 0 and m % tm == 0
    in_specs = [pl.BlockSpec((tm, k), lambda i, j: (i, 0)),
                pl.BlockSpec((None, k, tn), lambda i, j: (layer, 0, j))]
    args = [a, w]
    body = _mm_kernel
    if bias is not None:
        in_specs.append(pl.BlockSpec((1, tn), lambda i, j: (0, j)))
        args.append(bias.reshape(1, n))
        body = _mm_sigmoid_kernel
    return pl.pallas_call(
        body,
        grid=(m // tm, n // tn),
        in_specs=in_specs,
        out_specs=pl.BlockSpec((tm, tn), lambda i, j: (i, j)),
        out_shape=jax.ShapeDtypeStruct((m, n), out_dtype),
        compiler_params=_cparams(("parallel", "parallel"), 48),
        name="mm_sigmoid" if bias is not None else "mm",
    )(*args)


def _mm_kacc_kernel(a_ref, b_ref, o_ref):
    @pl.when(pl.program_id(1) == 0)
    def _():
        o_ref[...] = jnp.zeros_like(o_ref)

    o_ref[...] += jnp.dot(a_ref[...], b_ref[...], preferred_element_type=F32)


def matmul_kacc(a, w, layer, tm=MM_TILE, tk=MM_TILE):
    m, k = a.shape
    n = w.shape[2]
    return pl.pallas_call(
        _mm_kacc_kernel,
        grid=(m // tm, k // tk),
        in_specs=[pl.BlockSpec((tm, tk), lambda i, kk: (i, kk)),
                  pl.BlockSpec((None, tk, n), lambda i, kk: (layer, kk, 0))],
        out_specs=pl.BlockSpec((tm, n), lambda i, kk: (i, 0)),
        out_shape=jax.ShapeDtypeStruct((m, n), F32),
        compiler_params=_cparams(("parallel", "arbitrary"), 56),
        name="mm_kacc",
    )(a, w)


def _conv3(p, w_ref):
    t = lax.broadcasted_iota(jnp.int32, p.shape, 0)
    prev = jnp.where(t == 0, 0.0, pltpu.roll(p, 1, 0))
    nxt = jnp.where(t == SEQ - 1, 0.0, pltpu.roll(p, SEQ - 1, 0))
    return prev * w_ref[0:1, :] + p * w_ref[1:2, :] + nxt * w_ref[2:3, :]


def _short_conv_kernel(b_ref, c_ref, h_ref, w_ref, o_ref):
    p = c_ref[...].astype(F32) * h_ref[...].astype(F32)
    o_ref[...] = (b_ref[...].astype(F32) * _conv3(p, w_ref)).astype(o_ref.dtype)


def short_conv_mixer(hm, w_conv, bsz, tc=256):
    nb = BRANCH_WIDTH // tc

    def spec(off):
        return pl.BlockSpec((SEQ, tc), lambda b, j: (b, off // tc + j))

    return pl.pallas_call(
        _short_conv_kernel,
        grid=(bsz, nb),
        in_specs=[spec(OFF_CONV_B), spec(OFF_CONV_C), spec(OFF_CONV_H),
                  pl.BlockSpec((3, tc), lambda b, j: (0, j))],
        out_specs=pl.BlockSpec((SEQ, tc), lambda b, j: (b, j)),
        out_shape=jax.ShapeDtypeStruct((bsz * SEQ, BRANCH_WIDTH), BF16),
        compiler_params=_cparams(("parallel", "parallel"), 48),
        name="short_conv",
    )(hm, hm, hm, w_conv)


def _ffn_in_kernel(xp_ref, x_ref, xn_ref, wa_ref, wu_ref, wc_ref, bc_ref, o_ref, xext_ref, *, tiles_per_seq):
    i = pl.program_id(0)
    tm = x_ref.shape[0]
    n_ext = tm + 2 * FFN_HALO

    @pl.when(pl.program_id(1) == 0)
    def _():
        xext_ref[0:FFN_HALO] = xp_ref[...]
        xext_ref[FFN_HALO:FFN_HALO + tm] = x_ref[...]
        xext_ref[FFN_HALO + tm:n_ext] = xn_ref[...]

    a_ext = jnp.dot(xext_ref[...], wa_ref[...], preferred_element_type=F32)
    u = jnp.dot(x_ref[...], wu_ref[...], preferred_element_type=F32)
    a = a_ext[FFN_HALO:FFN_HALO + tm]
    prev = pltpu.roll(a_ext, 1, 0)[FFN_HALO:FFN_HALO + tm]
    nxt = pltpu.roll(a_ext, n_ext - 1, 0)[FFN_HALO:FFN_HALO + tm]
    pos = lax.rem(i, tiles_per_seq)
    row = lax.broadcasted_iota(jnp.int32, a.shape, 0)
    prev = jnp.where(row == jnp.where(pos == 0, 0, -1), 0.0, prev)
    nxt = jnp.where(row == jnp.where(pos == tiles_per_seq - 1, tm - 1, -1), 0.0, nxt)
    c = prev * wc_ref[0:1, :] + a * wc_ref[1:2, :] + nxt * wc_ref[2:3, :] + bc_ref[...]
    g = 0.5 * c * (1.0 + lax.erf(c * (2.0 ** -0.5)))
    o_ref[...] = (g * u).astype(o_ref.dtype)


def ffn_in(xb, w_ff, layer, w_conv, b_conv, tm=MM_TILE, tn=512):
    m, k = xb.shape
    nf = w_ff.shape[2] // 2
    nb = nf // tn
    hb = tm // FFN_HALO
    return pl.pallas_call(
        functools.partial(_ffn_in_kernel, tiles_per_seq=SEQ // tm),
        grid=(m // tm, nb),
        in_specs=[pl.BlockSpec((FFN_HALO, k), lambda i, j: (jnp.maximum(i * hb - 1, 0), 0)),
                  pl.BlockSpec((tm, k), lambda i, j: (i, 0)),
                  pl.BlockSpec((FFN_HALO, k), lambda i, j: (jnp.minimum((i + 1) * hb, m // FFN_HALO - 1), 0)),
                  pl.BlockSpec((None, k, tn), lambda i, j: (layer, 0, j)),
                  pl.BlockSpec((None, k, tn), lambda i, j: (layer, 0, nb + j)),
                  pl.BlockSpec((3, tn), lambda i, j: (0, j)),
                  pl.BlockSpec((1, tn), lambda i, j: (0, j))],
        out_specs=pl.BlockSpec((tm, tn), lambda i, j: (i, j)),
        out_shape=jax.ShapeDtypeStruct((m, nf), BF16),
        scratch_shapes=[pltpu.VMEM((tm + 2 * FFN_HALO, k), BF16)],
        compiler_params=_cparams(("parallel", "arbitrary"), 52),
        name="ffn_in",
    )(xb, xb, xb, w_ff, w_ff, w_conv, b_conv)


def _na_key_start(band):
    lo = band * NA_QROWS - NA_WIN_ROWS // 2
    hi = ROWS - NA_KROWS
    if isinstance(band, int):
        return min(max(lo, 0), hi)
    return jnp.clip(lo, 0, hi)


def _na_bias_tables(rpb):
    qc = np.arange(GRID_W)[:, None]
    kc = np.arange(GRID_W)[None, :]
    wc = np.clip(qc - NA_WIN_COLS // 2, 0, GRID_W - NA_WIN_COLS)
    col_valid = (kc >= wc) & (kc < wc + NA_WIN_COLS)
    cidx = np.clip(kc - qc + NA_WIN_COLS - 1, 0, 2 * NA_WIN_COLS - 2)
    tm = jnp.where(col_valid[None, None], rpb[:, :, cidx], NEG_INF)
    tm = jnp.concatenate([tm, jnp.full_like(tm[:, :1], NEG_INF)], axis=1)
    nbands = ROWS // NA_QROWS
    blk = np.zeros((3, NA_QROWS, NA_KROWS), np.int32)
    for c, band in enumerate((0, 1, nbands - 1)):
        k0 = _na_key_start(band)
        for i in range(NA_QROWS):
            r = band * NA_QROWS + i
            wr = min(max(r - NA_WIN_ROWS // 2, 0), ROWS - NA_WIN_ROWS)
            for j in range(NA_KROWS):
                kr = k0 + j
                blk[c, i, j] = kr - r + NA_WIN_ROWS - 1 if wr <= kr < wr + NA_WIN_ROWS else 2 * NA_WIN_ROWS - 1
    t = tm[:, blk]
    t = jnp.transpose(t, (0, 1, 2, 4, 3, 5))
    return t.reshape(rpb.shape[0], 3, NA_QROWS * GRID_W, NA_KROWS * GRID_W)


def _na_kernel(q_ref, k_ref, v_ref, bias_ref, o_ref):
    band = pl.program_id(2)
    k0 = pl.multiple_of(_na_key_start(band) * GRID_W, GRID_W)
    nk = NA_KROWS * GRID_W
    for h in range(NA_HEADS_PER_STEP):
        cs = slice(h * NA_HEAD_DIM, (h + 1) * NA_HEAD_DIM)
        k = k_ref[pl.ds(k0, nk), cs]
        v = v_ref[pl.ds(k0, nk), cs]
        s = lax.dot_general(q_ref[:, cs], k, (((1,), (1,)), ((), ())), preferred_element_type=F32)
        s = s * (NA_HEAD_DIM ** -0.5) + bias_ref[h, 0]
        m = jnp.max(s, axis=-1, keepdims=True)
        p = jnp.exp(s - m)
        l = jnp.sum(p, axis=-1, keepdims=True)
        o = jnp.dot(p.astype(BF16), v, preferred_element_type=F32)
        o_ref[:, cs] = (o / l).astype(o_ref.dtype)


def neighbourhood_attention(hm, bias_tab, bsz):
    nbands = ROWS // NA_QROWS
    tq = NA_QROWS * GRID_W
    hp = NA_HEADS_PER_STEP
    wg = hp * NA_HEAD_DIM
    qo, ko, vo = (off // wg for off in (OFF_NA_Q, OFF_NA_K, OFF_NA_V))
    return pl.pallas_call(
        _na_kernel,
        grid=(NA_HEADS // hp, bsz, nbands),
        in_specs=[
            pl.BlockSpec((tq, wg), lambda g, b, r: (b * nbands + r, qo + g)),
            pl.BlockSpec((SEQ, wg), lambda g, b, r: (b, ko + g)),
            pl.BlockSpec((SEQ, wg), lambda g, b, r: (b, vo + g)),
            pl.BlockSpec((hp, 1, tq, NA_KROWS * GRID_W),
                         lambda g, b, r: (g, jnp.minimum(r, 1) + r // (nbands - 1), 0, 0)),
        ],
        out_specs=pl.BlockSpec((tq, wg), lambda g, b, r: (b * nbands + r, g)),
        out_shape=jax.ShapeDtypeStruct((bsz * SEQ, BRANCH_WIDTH), BF16),
        compiler_params=_cparams(("parallel", "parallel", "parallel"), 48),
        name="natten",
    )(hm, hm, hm, bias_tab)


def _gla_scan_kernel(qf_ref, kf_ref, vf_ref, af_ref, qb_ref, kb_ref, vb_ref, ab_ref,
                     wdec_ref, bdec_ref, of_ref, ob_ref, state_ref):
    @pl.when(pl.program_id(1) == 0)
    def _():
        state_ref[...] = jnp.zeros_like(state_ref)

    row = lax.broadcasted_iota(jnp.int32, (GLA_CHUNK, GLA_CHUNK), 0)
    col = lax.broadcasted_iota(jnp.int32, (GLA_CHUNK, GLA_CHUNK), 1)
    nt = (((1,), (1,)), ((), ()))
    tn = (((0,), (0,)), ((), ()))
    dirs = ((qf_ref, kf_ref, vf_ref, af_ref, of_ref, row >= col, GLA_CHUNK - 1),
            (qb_ref, kb_ref, vb_ref, ab_ref, ob_ref, row <= col, 0))
    for d, (q_ref, k_ref, v_ref, a_ref, o_ref, tri, last_row) in enumerate(dirs):
        z = jnp.dot(a_ref[...].astype(BF16), wdec_ref[d], preferred_element_type=F32) + bdec_ref[d:d + 1, :]
        log_a = (jnp.minimum(z, 0.0) - jnp.log1p(jnp.exp(-jnp.abs(z)))) * (1.0 / GLA_GATE_TAU)
        cum = jnp.dot(tri.astype(F32), log_a, precision=lax.Precision.HIGHEST, preferred_element_type=F32)
        last = cum[last_row:last_row + 1, :]
        q_in = q_ref[...].astype(F32) * ((GLA_KEY_DIM ** -0.5) * jnp.exp(cum))
        kk = k_ref[...].astype(F32)
        k_in = kk * jnp.exp(-cum)
        k_out = kk * jnp.exp(last - cum)
        dec = jnp.exp(last)
        for h in range(GLA_HEADS):
            ks = slice(h * GLA_KEY_DIM, (h + 1) * GLA_KEY_DIM)
            vs = slice(h * GLA_VALUE_DIM, (h + 1) * GLA_VALUE_DIM)
            qh = q_in[:, ks].astype(BF16)
            vh = v_ref[:, vs]
            att = lax.dot_general(qh, k_in[:, ks].astype(BF16), nt, preferred_element_type=F32)
            att = jnp.where(tri, att, 0.0).astype(BF16)
            st = state_ref[d, h]
            o = jnp.dot(att, vh, preferred_element_type=F32)
            o = o + lax.dot_general(qh, st.astype(BF16), nt, preferred_element_type=F32)
            o_ref[:, vs] = o
            upd = lax.dot_general(vh, k_out[:, ks].astype(BF16), tn, preferred_element_type=F32)
            state_ref[d, h] = st * dec[:, ks] + upd


def gla_scan(hm, a_lr, wdec, bdec, bsz):
    qo = OFF_GLA_Q // GLA_KEY_WIDTH
    ko = OFF_GLA_K // GLA_KEY_WIDTH
    vo = OFF_GLA_V // BRANCH_WIDTH

    def fwd(colblk):
        return lambda b, c: (b * N_CHUNKS + c, colblk)

    def bwd(colblk):
        return lambda b, c: (b * N_CHUNKS + N_CHUNKS - 1 - c, colblk)

    def specs(idx):
        return [pl.BlockSpec((GLA_CHUNK, GLA_KEY_WIDTH), idx(qo)),
                pl.BlockSpec((GLA_CHUNK, GLA_KEY_WIDTH), idx(ko)),
                pl.BlockSpec((GLA_CHUNK, BRANCH_WIDTH), idx(vo)),
                pl.BlockSpec((GLA_CHUNK, LANES), idx(0))]

    out = jax.ShapeDtypeStruct((bsz * SEQ, BRANCH_WIDTH), F32)
    return pl.pallas_call(
        _gla_scan_kernel,
        grid=(bsz, N_CHUNKS),
        in_specs=specs(fwd) + specs(bwd) + [
            pl.BlockSpec((2, LANES, GLA_KEY_WIDTH), lambda b, c: (0, 0, 0)),
            pl.BlockSpec((2, GLA_KEY_WIDTH), lambda b, c: (0, 0))],
        out_specs=[pl.BlockSpec((GLA_CHUNK, BRANCH_WIDTH), fwd(0)),
                   pl.BlockSpec((GLA_CHUNK, BRANCH_WIDTH), bwd(0))],
        out_shape=[out, out],
        scratch_shapes=[pltpu.VMEM((2, GLA_HEADS, GLA_VALUE_DIM, GLA_KEY_DIM), F32)],
        compiler_params=_cparams(("parallel", "arbitrary"), 32),
        name="gla_scan",
    )(hm, hm, hm, a_lr, hm, hm, hm, a_lr, wdec, bdec)


def _gla_out_kernel(of_ref, ob_ref, r_ref, g_ref, o_ref):
    for h in range(GLA_HEADS):
        vs = slice(h * GLA_VALUE_DIM, (h + 1) * GLA_VALUE_DIM)
        o = of_ref[:, vs] + ob_ref[:, vs]
        o = o * lax.rsqrt(jnp.mean(o * o, axis=-1, keepdims=True) + RMS_EPS) * g_ref[:, vs]
        r = r_ref[:, vs].astype(F32)
        o_ref[:, vs] = (o * (r * jax.nn.sigmoid(r))).astype(o_ref.dtype)


def gla_output(o_f, o_b, hm, norm_g, tm=512):
    m = o_f.shape[0]
    row = pl.BlockSpec((tm, BRANCH_WIDTH), lambda i: (i, 0))
    return pl.pallas_call(
        _gla_out_kernel,
        grid=(m // tm,),
        in_specs=[row, row,
                  pl.BlockSpec((tm, BRANCH_WIDTH), lambda i: (i, OFF_GLA_R // BRANCH_WIDTH)),
                  pl.BlockSpec((1, BRANCH_WIDTH), lambda i: (0, 0))],
        out_specs=row,
        out_shape=jax.ShapeDtypeStruct((m, BRANCH_WIDTH), BF16),
        compiler_params=_cparams(("parallel",), 40),
        name="gla_output",
    )(o_f, o_b, hm, norm_g.reshape(1, BRANCH_WIDTH))


def _mix_kernel(y0_ref, y1_ref, y2_ref, w_ref, g0_ref, g1_ref, g2_ref, o_ref):
    acc = None
    for i, (y_ref, g_ref) in enumerate(((y0_ref, g0_ref), (y1_ref, g1_ref), (y2_ref, g2_ref))):
        t = g_ref[...].astype(F32) * jnp.dot(y_ref[...], w_ref[i], preferred_element_type=F32)
        acc = t if acc is None else acc + t
    o_ref[...] = acc.astype(o_ref.dtype)


def gated_mix(ys, w_branch, layer, gates, tm=MM_TILE, tn=512):
    m = ys[0].shape[0]
    nj = D_MODEL // tn
    yspec = pl.BlockSpec((tm, BRANCH_WIDTH), lambda i, j: (i, 0))

    def gspec(br):
        return pl.BlockSpec((tm, tn), lambda i, j: (i, br * nj + j))

    return pl.pallas_call(
        _mix_kernel,
        grid=(m // tm, nj),
        in_specs=[yspec, yspec, yspec,
                  pl.BlockSpec((None, N_BRANCH, BRANCH_WIDTH, tn), lambda i, j: (layer, 0, 0, j)),
                  gspec(0), gspec(1), gspec(2)],
        out_specs=pl.BlockSpec((tm, tn), lambda i, j: (i, j)),
        out_shape=jax.ShapeDtypeStruct((m, D_MODEL), BF16),
        compiler_params=_cparams(("parallel", "parallel"), 52),
        name="gated_mix",
    )(*ys, w_branch, gates, gates, gates)


def _ln_kernel(x_ref, s_ref, g_ref, b_ref, of_ref, ob_ref, *, alpha):
    y = alpha * x_ref[...] + s_ref[...]
    mu = jnp.mean(y, axis=-1, keepdims=True)
    yc = y - mu
    var = jnp.mean(yc * yc, axis=-1, keepdims=True)
    out = yc * lax.rsqrt(var + LN_EPS) * g_ref[...] + b_ref[...]
    of_ref[...] = out
    ob_ref[...] = out.astype(BF16)


def residual_layer_norm(x, s, g, b, alpha, tm=256):
    m, d = x.shape
    row = pl.BlockSpec((tm, d), lambda i: (i, 0))
    vec = pl.BlockSpec((1, d), lambda i: (0, 0))
    return pl.pallas_call(
        functools.partial(_ln_kernel, alpha=alpha),
        grid=(m // tm,),
        in_specs=[row, row, vec, vec],
        out_specs=[row, row],
        out_shape=[jax.ShapeDtypeStruct((m, d), F32), jax.ShapeDtypeStruct((m, d), BF16)],
        compiler_params=_cparams(("parallel",), 40),
        name="residual_ln",
    )(x, s, g.reshape(1, d), b.reshape(1, d))


def kernel(x, w_in, b_gate, w_short_conv, na_rpb, gla_w_decay, gla_b_decay, gla_norm_g, w_branch, w_out,
           ln1_g, ln1_b, w_ffn_in, w_ffn_conv, b_ffn_conv, w_ffn_out, ln2_g, ln2_b):
    bsz, s, d = x.shape
    depth = w_in.shape[0]
    assert (s, d) == (SEQ, D_MODEL) and w_in.shape[2] == OFF_GATE + N_BRANCH * D_MODEL
    assert w_ffn_in.shape[2] == 2 * D_FF
    alpha = (2.0 * depth) ** 0.25
    ff_pad = D_FF_PAD - D_FF

    w_main_b, w_gate_b, w_dec_b = prep_w_in(w_in)
    w_ff_b, w_ffo_b = prep_w_ffn(w_ffn_in, w_ffn_out)
    w_br_b = w_branch.astype(BF16)
    w_out_b = w_out.astype(BF16)
    w_conv_ff = jnp.pad(w_ffn_conv, ((0, 0), (0, 0), (0, ff_pad)))
    b_conv_ff = jnp.pad(b_ffn_conv, ((0, 0), (0, ff_pad)))

    xf = x.reshape(bsz * s, d)
    xb = xf.astype(BF16)
    for l in range(depth):
        hm = matmul(xb, w_main_b, l, BF16)
        gates = matmul(xb, w_gate_b, l, BF16, bias=b_gate[l])
        a_lr = matmul(xb, w_dec_b, l, F32)

        y_conv = short_conv_mixer(hm, w_short_conv[l], bsz)
        y_na = neighbourhood_attention(hm, _na_bias_tables(na_rpb[l]), bsz)
        wdec = jnp.zeros((2, LANES, GLA_KEY_WIDTH), F32)
        wdec = wdec.at[0, :GLA_GATE_RANK].set(gla_w_decay[l, 0])
        wdec = wdec.at[1, GLA_GATE_RANK:2 * GLA_GATE_RANK].set(gla_w_decay[l, 1])
        o_f, o_b = gla_scan(hm, a_lr, wdec.astype(BF16), gla_b_decay[l], bsz)
        y_gla = gla_output(o_f, o_b, hm, gla_norm_g[l])

        mixed = gated_mix((y_conv, y_na, y_gla), w_br_b, l, gates)
        s1 = matmul(mixed, w_out_b, l, F32)
        xf, xb = residual_layer_norm(xf, s1, ln1_g[l], ln1_b[l], alpha)

        g = ffn_in(xb, w_ff_b, l, w_conv_ff[l], b_conv_ff[l].reshape(1, D_FF_PAD))
        s2 = matmul_kacc(g, w_ffo_b, l)
        xf, xb = residual_layer_norm(xf, s2, ln2_g[l], ln2_b[l], alpha)
    return xf.reshape(bsz, s, d)
```

```python
import functools

import numpy as np
import jax
import jax.numpy as jnp
from jax import lax
from jax.experimental import pallas as pl
from jax.experimental.pallas import tpu as pltpu

F32 = jnp.float32
BF16 = jnp.bfloat16

D_MODEL = 4096
SEQ = 4096
GRID_W = 64
ROWS = SEQ // GRID_W
BRANCH_WIDTH = D_MODEL // 2
N_BRANCH = 3
NA_HEAD_DIM = 128
NA_HEADS = BRANCH_WIDTH // NA_HEAD_DIM
NA_WIN_ROWS = 8
NA_WIN_COLS = 16
NA_QROWS = 8
NA_KROWS = 16
NA_HEADS_PER_STEP = 4
GLA_HEADS = 4
GLA_KEY_WIDTH = BRANCH_WIDTH // 2
GLA_KEY_DIM = GLA_KEY_WIDTH // GLA_HEADS
GLA_VALUE_DIM = BRANCH_WIDTH // GLA_HEADS
GLA_GATE_RANK = 16
GLA_GATE_TAU = 16.0
GLA_CHUNK = 64
N_CHUNKS = SEQ // GLA_CHUNK
D_FF = 256 * ((8 * D_MODEL // 3 + 255) // 256)
LN_EPS = 1e-5
RMS_EPS = 1e-6
NEG_INF = -1e30

OFF_CONV_B = 0
OFF_CONV_C = BRANCH_WIDTH
OFF_CONV_H = 2 * BRANCH_WIDTH
OFF_NA_Q = 3 * BRANCH_WIDTH
OFF_NA_K = 4 * BRANCH_WIDTH
OFF_NA_V = 5 * BRANCH_WIDTH
OFF_GLA_Q = 6 * BRANCH_WIDTH
OFF_GLA_K = OFF_GLA_Q + GLA_KEY_WIDTH
OFF_GLA_V = OFF_GLA_K + GLA_KEY_WIDTH
OFF_GLA_R = OFF_GLA_V + BRANCH_WIDTH
MAIN_WIDTH = OFF_GLA_R + BRANCH_WIDTH
OFF_GATE = MAIN_WIDTH + 2 * GLA_GATE_RANK

LANES = 128
V7X_VMEM_BYTES = 64 * 1024 * 1024
MM_TILE = 1024
D_FF_PAD = MM_TILE * (-(-D_FF // MM_TILE))
FFN_HALO = 16
MIB = 1024 * 1024


def _cparams(semantics, vmem_mib):
    assert vmem_mib * MIB < V7X_VMEM_BYTES
    return pltpu.CompilerParams(dimension_semantics=semantics, vmem_limit_bytes=vmem_mib * MIB)


def _cast_kernel(src_ref, o_ref):
    o_ref[...] = src_ref[...].astype(o_ref.dtype)


def _cast_or_zero_kernel(src_ref, o_ref, *, axis, pad_index):
    is_pad = pl.program_id(axis) % (pad_index + 1) == pad_index

    @pl.when(is_pad)
    def _():
        o_ref[...] = jnp.zeros_like(o_ref)

    @pl.when(jnp.logical_not(is_pad))
    def _():
        o_ref[...] = src_ref[...].astype(o_ref.dtype)


def _row_shift_cast_kernel(src_ref, tail_ref, o_ref):
    rows, shift = src_ref.shape[0], tail_ref.shape[0]
    o_ref[0:rows - shift] = src_ref[shift:rows].astype(o_ref.dtype)
    o_ref[rows - shift:rows] = tail_ref[...].astype(o_ref.dtype)


def _head_rows_cast_kernel(src_ref, o_ref, *, keep):
    row = lax.broadcasted_iota(jnp.int32, src_ref.shape, 0)
    o_ref[...] = jnp.where(row < keep, src_ref[...], 0.0).astype(o_ref.dtype)


def prep_w_in(w_in_t, tr=512):
    depth, _, d = w_in_t.shape
    n_gate = N_BRANCH * D_MODEL
    dec_w = 2 * GLA_GATE_RANK
    params = _cparams(("parallel", "parallel"), 40)
    blk = (None, tr, d)
    main = pl.pallas_call(
        _cast_kernel, grid=(depth, MAIN_WIDTH // tr),
        in_specs=[pl.BlockSpec(blk, lambda l, j: (l, j, 0))],
        out_specs=pl.BlockSpec(blk, lambda l, j: (l, j, 0)),
        out_shape=jax.ShapeDtypeStruct((depth, MAIN_WIDTH, d), BF16),
        compiler_params=params, name="prep_w_main")(w_in_t)
    jb = MAIN_WIDTH // tr
    gates = pl.pallas_call(
        _row_shift_cast_kernel, grid=(depth, n_gate // tr),
        in_specs=[pl.BlockSpec(blk, lambda l, j: (l, jb + j, 0)),
                  pl.BlockSpec((None, dec_w, d), lambda l, j: (l, (jb + j + 1) * (tr // dec_w), 0))],
        out_specs=pl.BlockSpec(blk, lambda l, j: (l, j, 0)),
        out_shape=jax.ShapeDtypeStruct((depth, n_gate, d), BF16),
        compiler_params=params, name="prep_w_gate")(w_in_t, w_in_t)
    dec = pl.pallas_call(
        functools.partial(_head_rows_cast_kernel, keep=dec_w), grid=(depth,),
        in_specs=[pl.BlockSpec((None, LANES, d), lambda l: (l, MAIN_WIDTH // LANES, 0))],
        out_specs=pl.BlockSpec((None, LANES, d), lambda l: (l, 0, 0)),
        out_shape=jax.ShapeDtypeStruct((depth, LANES, d), BF16),
        compiler_params=_cparams(("parallel",), 32), name="prep_w_dec")(w_in_t)
    return main, gates, dec


def prep_w_ffn(w_ffn_in, w_ffn_out, tb=256):
    depth, d, _ = w_ffn_in.shape
    live, padded = D_FF // tb, D_FF_PAD // tb
    assert live * tb == D_FF and padded == live + 1
    w_in_b = pl.pallas_call(
        functools.partial(_cast_or_zero_kernel, axis=1, pad_index=live),
        grid=(depth, 2 * padded),
        in_specs=[pl.BlockSpec((None, d, tb),
                               lambda l, j: (l, 0, (j // padded) * live + jnp.minimum(j % padded, live - 1)))],
        out_specs=pl.BlockSpec((None, d, tb), lambda l, j: (l, 0, j)),
        out_shape=jax.ShapeDtypeStruct((depth, d, 2 * D_FF_PAD), BF16),
        compiler_params=_cparams(("parallel", "parallel"), 32), name="prep_w_ffn_in")(w_ffn_in)
    w_out_b = pl.pallas_call(
        functools.partial(_cast_or_zero_kernel, axis=1, pad_index=live),
        grid=(depth, padded),
        in_specs=[pl.BlockSpec((None, tb, d), lambda l, i: (l, jnp.minimum(i, live - 1), 0))],
        out_specs=pl.BlockSpec((None, tb, d), lambda l, i: (l, i, 0)),
        out_shape=jax.ShapeDtypeStruct((depth, D_FF_PAD, d), BF16),
        compiler_params=_cparams(("parallel", "parallel"), 32), name="prep_w_ffn_out")(w_ffn_out)
    return w_in_b, w_out_b


_NT = (((1,), (1,)), ((), ()))


def _mm_kernel(a_ref, b_ref, o_ref, *, dims):
    acc = lax.dot_general(a_ref[...], b_ref[...], dims, preferred_element_type=F32)
    o_ref[...] = acc.astype(o_ref.dtype)


def _mm_sigmoid_kernel(a_ref, b_ref, bias_ref, o_ref, *, dims):
    acc = lax.dot_general(a_ref[...], b_ref[...], dims, preferred_element_type=F32)
    o_ref[...] = jax.nn.sigmoid(acc + bias_ref[...]).astype(o_ref.dtype)


def _mm_residual_kernel(a_ref, b_ref, res_ref, o_ref, *, dims, alpha):
    acc = lax.dot_general(a_ref[...], b_ref[...], dims, preferred_element_type=F32)
    o_ref[...] = (alpha * res_ref[...] + acc).astype(o_ref.dtype)


def matmul(a, w, layer, out_dtype, bias=None, residual=None, alpha=None, w_transposed=False,
           tm=MM_TILE, tn=MM_TILE):
    m, k = a.shape
    n = w.shape[1] if w_transposed else w.shape[2]
    tn = min(tn, n)
    assert n % tn == 0 and m % tm == 0
    if w_transposed:
        w_spec = pl.BlockSpec((None, tn, k), lambda i, j: (layer, j, 0))
        dims = _NT
    else:
        w_spec = pl.BlockSpec((None, k, tn), lambda i, j: (layer, 0, j))
        dims = (((1,), (0,)), ((), ()))
    in_specs = [pl.BlockSpec((tm, k), lambda i, j: (i, 0)), w_spec]
    args = [a, w]
    body = _mm_kernel
    if bias is not None:
        in_specs.append(pl.BlockSpec((1, tn), lambda i, j: (0, j)))
        args.append(bias.reshape(1, n))
        body = _mm_sigmoid_kernel
        name = "mm_sigmoid"
    elif residual is not None:
        in_specs.append(pl.BlockSpec((tm, tn), lambda i, j: (i, j)))
        args.append(residual)
        body = functools.partial(_mm_residual_kernel, alpha=alpha)
        name = "mm_residual"
    else:
        name = "mm"
    return pl.pallas_call(
        functools.partial(body, dims=dims),
        grid=(m // tm, n // tn),
        in_specs=in_specs,
        out_specs=pl.BlockSpec((tm, tn), lambda i, j: (i, j)),
        out_shape=jax.ShapeDtypeStruct((m, n), out_dtype),
        compiler_params=_cparams(("parallel", "parallel"), 52),
        name=name,
    )(*args)


def _mm_kacc_kernel(a_ref, b_ref, o_ref):
    @pl.when(pl.program_id(1) == 0)
    def _():
        o_ref[...] = jnp.zeros_like(o_ref)

    o_ref[...] += jnp.dot(a_ref[...], b_ref[...], preferred_element_type=F32)


def matmul_kacc(a, w, layer, tm=MM_TILE, tk=MM_TILE):
    m, k = a.shape
    n = w.shape[2]
    return pl.pallas_call(
        _mm_kacc_kernel,
        grid=(m // tm, k // tk),
        in_specs=[pl.BlockSpec((tm, tk), lambda i, kk: (i, kk)),
                  pl.BlockSpec((None, tk, n), lambda i, kk: (layer, kk, 0))],
        out_specs=pl.BlockSpec((tm, n), lambda i, kk: (i, 0)),
        out_shape=jax.ShapeDtypeStruct((m, n), F32),
        compiler_params=_cparams(("parallel", "arbitrary"), 56),
        name="mm_kacc",
    )(a, w)


def _conv3(p, w_ref):
    t = lax.broadcasted_iota(jnp.int32, p.shape, 0)
    prev = jnp.where(t == 0, 0.0, pltpu.roll(p, 1, 0))
    nxt = jnp.where(t == SEQ - 1, 0.0, pltpu.roll(p, SEQ - 1, 0))
    return prev * w_ref[0:1, :] + p * w_ref[1:2, :] + nxt * w_ref[2:3, :]


def _short_conv_kernel(b_ref, c_ref, h_ref, w_ref, o_ref):
    p = c_ref[...].astype(F32) * h_ref[...].astype(F32)
    o_ref[...] = (b_ref[...].astype(F32) * _conv3(p, w_ref)).astype(o_ref.dtype)


def short_conv_mixer(hm, w_conv, bsz, tc=256):
    nb = BRANCH_WIDTH // tc

    def spec(off):
        return pl.BlockSpec((SEQ, tc), lambda b, j: (b, off // tc + j))

    return pl.pallas_call(
        _short_conv_kernel,
        grid=(bsz, nb),
        in_specs=[spec(OFF_CONV_B), spec(OFF_CONV_C), spec(OFF_CONV_H),
                  pl.BlockSpec((3, tc), lambda b, j: (0, j))],
        out_specs=pl.BlockSpec((SEQ, tc), lambda b, j: (b, j)),
        out_shape=jax.ShapeDtypeStruct((bsz * SEQ, BRANCH_WIDTH), BF16),
        compiler_params=_cparams(("parallel", "parallel"), 48),
        name="short_conv",
    )(hm, hm, hm, w_conv)


def _ffn_in_kernel(xp_ref, x_ref, xn_ref, wa_ref, wu_ref, wc_ref, bc_ref, o_ref, xext_ref, *, tiles_per_seq):
    i = pl.program_id(0)
    tm = x_ref.shape[0]
    n_ext = tm + 2 * FFN_HALO

    @pl.when(pl.program_id(1) == 0)
    def _():
        xext_ref[0:FFN_HALO] = xp_ref[...]
        xext_ref[FFN_HALO:FFN_HALO + tm] = x_ref[...]
        xext_ref[FFN_HALO + tm:n_ext] = xn_ref[...]

    a_ext = jnp.dot(xext_ref[...], wa_ref[...], preferred_element_type=F32)
    u = jnp.dot(x_ref[...], wu_ref[...], preferred_element_type=F32)
    a = a_ext[FFN_HALO:FFN_HALO + tm]
    prev = pltpu.roll(a_ext, 1, 0)[FFN_HALO:FFN_HALO + tm]
    nxt = pltpu.roll(a_ext, n_ext - 1, 0)[FFN_HALO:FFN_HALO + tm]
    pos = lax.rem(i, tiles_per_seq)
    row = lax.broadcasted_iota(jnp.int32, a.shape, 0)
    prev = jnp.where(row == jnp.where(pos == 0, 0, -1), 0.0, prev)
    nxt = jnp.where(row == jnp.where(pos == tiles_per_seq - 1, tm - 1, -1), 0.0, nxt)
    c = prev * wc_ref[0:1, :] + a * wc_ref[1:2, :] + nxt * wc_ref[2:3, :] + bc_ref[...]
    g = 0.5 * c * (1.0 + lax.erf(c * (2.0 ** -0.5)))
    o_ref[...] = (g * u).astype(o_ref.dtype)


def ffn_in(xb, w_ff, layer, w_conv, b_conv, tm=MM_TILE, tn=512):
    m, k = xb.shape
    nf = w_ff.shape[2] // 2
    nb = nf // tn
    hb = tm // FFN_HALO
    return pl.pallas_call(
        functools.partial(_ffn_in_kernel, tiles_per_seq=SEQ // tm),
        grid=(m // tm, nb),
        in_specs=[pl.BlockSpec((FFN_HALO, k), lambda i, j: (jnp.maximum(i * hb - 1, 0), 0)),
                  pl.BlockSpec((tm, k), lambda i, j: (i, 0)),
                  pl.BlockSpec((FFN_HALO, k), lambda i, j: (jnp.minimum((i + 1) * hb, m // FFN_HALO - 1), 0)),
                  pl.BlockSpec((None, k, tn), lambda i, j: (layer, 0, j)),
                  pl.BlockSpec((None, k, tn), lambda i, j: (layer, 0, nb + j)),
                  pl.BlockSpec((3, tn), lambda i, j: (0, j)),
                  pl.BlockSpec((1, tn), lambda i, j: (0, j))],
        out_specs=pl.BlockSpec((tm, tn), lambda i, j: (i, j)),
        out_shape=jax.ShapeDtypeStruct((m, nf), BF16),
        scratch_shapes=[pltpu.VMEM((tm + 2 * FFN_HALO, k), BF16)],
        compiler_params=_cparams(("parallel", "arbitrary"), 52),
        name="ffn_in",
    )(xb, xb, xb, w_ff, w_ff, w_conv, b_conv)


def _na_key_start(band):
    lo = band * NA_QROWS - NA_WIN_ROWS // 2
    hi = ROWS - NA_KROWS
    if isinstance(band, int):
        return min(max(lo, 0), hi)
    return jnp.clip(lo, 0, hi)


NA_ROW_OFFSETS = 2 * NA_WIN_ROWS - 1
NA_PAIR_CODES = 3 * NA_ROW_OFFSETS


def _na_pair_tables(rpb):
    qc = np.arange(GRID_W)[:, None]
    kc = np.arange(GRID_W)[None, :]
    wc = np.clip(qc - NA_WIN_COLS // 2, 0, GRID_W - NA_WIN_COLS)
    col_valid = (kc >= wc) & (kc < wc + NA_WIN_COLS)
    cidx = np.clip(kc - qc + NA_WIN_COLS - 1, 0, 2 * NA_WIN_COLS - 2)
    tm = jnp.where(col_valid[None, None], rpb[:, :, cidx], NEG_INF)
    tm = jnp.concatenate([tm, jnp.full_like(tm[:, :1], NEG_INF)], axis=1)
    out = NA_ROW_OFFSETS
    both = np.arange(out - 1)
    one = np.arange(out)
    left = np.concatenate([both, one, np.full(out, out), [out]])
    right = np.concatenate([both + 1, np.full(out, out), one, [out]])
    return jnp.concatenate([tm[:, left], tm[:, right]], axis=-1)


def _na_pair_code(band, i, jp):
    r = band * NA_QROWS + i
    wr = jnp.clip(r - NA_WIN_ROWS // 2, 0, ROWS - NA_WIN_ROWS)
    kl = _na_key_start(band) + 2 * jp
    dl = kl - r + NA_WIN_ROWS - 1
    vl = jnp.logical_and(kl >= wr, kl < wr + NA_WIN_ROWS)
    vr = jnp.logical_and(kl + 1 >= wr, kl + 1 < wr + NA_WIN_ROWS)
    out = NA_ROW_OFFSETS
    return jnp.where(jnp.logical_and(vl, vr), dl,
                     jnp.where(vl, out - 1 + dl, jnp.where(vr, 2 * out - 1 + dl + 1, 3 * out - 1)))


def _na_kernel(q_ref, k_ref, v_ref, bias_ref, o_ref):
    band = pl.program_id(2)
    k0 = pl.multiple_of(_na_key_start(band) * GRID_W, GRID_W)
    nk = NA_KROWS * GRID_W
    codes = [[_na_pair_code(band, i, jp) for jp in range(NA_KROWS // 2)] for i in range(NA_QROWS)]
    for h in range(NA_HEADS_PER_STEP):
        cs = slice(h * NA_HEAD_DIM, (h + 1) * NA_HEAD_DIM)
        k = k_ref[pl.ds(k0, nk), cs]
        v = v_ref[pl.ds(k0, nk), cs]
        bias = jnp.concatenate([jnp.concatenate([bias_ref[h, c] for c in row], axis=1) for row in codes], axis=0)
        s = lax.dot_general(q_ref[:, cs], k, (((1,), (1,)), ((), ())), preferred_element_type=F32)
        s = s * (NA_HEAD_DIM ** -0.5) + bias
        m = jnp.max(s, axis=-1, keepdims=True)
        p = jnp.exp(s - m)
        l = jnp.sum(p, axis=-1, keepdims=True)
        o = jnp.dot(p.astype(BF16), v, preferred_element_type=F32)
        o_ref[:, cs] = (o / l).astype(o_ref.dtype)


def neighbourhood_attention(hm, bias_tab, bsz):
    nbands = ROWS // NA_QROWS
    tq = NA_QROWS * GRID_W
    hp = NA_HEADS_PER_STEP
    wg = hp * NA_HEAD_DIM
    qo, ko, vo = (off // wg for off in (OFF_NA_Q, OFF_NA_K, OFF_NA_V))
    return pl.pallas_call(
        _na_kernel,
        grid=(NA_HEADS // hp, bsz, nbands),
        in_specs=[
            pl.BlockSpec((tq, wg), lambda g, b, r: (b * nbands + r, qo + g)),
            pl.BlockSpec((SEQ, wg), lambda g, b, r: (b, ko + g)),
            pl.BlockSpec((SEQ, wg), lambda g, b, r: (b, vo + g)),
            pl.BlockSpec((hp, NA_PAIR_CODES, GRID_W, 2 * GRID_W), lambda g, b, r: (g, 0, 0, 0)),
        ],
        out_specs=pl.BlockSpec((tq, wg), lambda g, b, r: (b * nbands + r, g)),
        out_shape=jax.ShapeDtypeStruct((bsz * SEQ, BRANCH_WIDTH), BF16),
        compiler_params=_cparams(("parallel", "parallel", "parallel"), 48),
        name="natten",
    )(hm, hm, hm, bias_tab)


def _gla_scan_kernel(qf_ref, kf_ref, vf_ref, af_ref, qb_ref, kb_ref, vb_ref, ab_ref,
                     wdec_ref, bdec_ref, of_ref, ob_ref, state_ref):
    @pl.when(pl.program_id(1) == 0)
    def _():
        state_ref[...] = jnp.zeros_like(state_ref)

    row = lax.broadcasted_iota(jnp.int32, (GLA_CHUNK, GLA_CHUNK), 0)
    col = lax.broadcasted_iota(jnp.int32, (GLA_CHUNK, GLA_CHUNK), 1)
    nt = (((1,), (1,)), ((), ()))
    tn = (((0,), (0,)), ((), ()))
    dirs = ((qf_ref, kf_ref, vf_ref, af_ref, of_ref, row >= col, GLA_CHUNK - 1),
            (qb_ref, kb_ref, vb_ref, ab_ref, ob_ref, row <= col, 0))
    for d, (q_ref, k_ref, v_ref, a_ref, o_ref, tri, last_row) in enumerate(dirs):
        z = jnp.dot(a_ref[...].astype(BF16), wdec_ref[d], preferred_element_type=F32) + bdec_ref[d:d + 1, :]
        log_a = (jnp.minimum(z, 0.0) - jnp.log1p(jnp.exp(-jnp.abs(z)))) * (1.0 / GLA_GATE_TAU)
        cum = jnp.dot(tri.astype(F32), log_a, precision=lax.Precision.HIGHEST, preferred_element_type=F32)
        last = cum[last_row:last_row + 1, :]
        q_in = q_ref[...].astype(F32) * ((GLA_KEY_DIM ** -0.5) * jnp.exp(cum))
        kk = k_ref[...].astype(F32)
        k_in = kk * jnp.exp(-cum)
        k_out = kk * jnp.exp(last - cum)
        dec = jnp.exp(last)
        for h in range(GLA_HEADS):
            ks = slice(h * GLA_KEY_DIM, (h + 1) * GLA_KEY_DIM)
            vs = slice(h * GLA_VALUE_DIM, (h + 1) * GLA_VALUE_DIM)
            qh = q_in[:, ks].astype(BF16)
            vh = v_ref[:, vs]
            att = lax.dot_general(qh, k_in[:, ks].astype(BF16), nt, preferred_element_type=F32)
            att = jnp.where(tri, att, 0.0).astype(BF16)
            st = state_ref[d, h]
            o = jnp.dot(att, vh, preferred_element_type=F32)
            o = o + lax.dot_general(qh, st.astype(BF16), nt, preferred_element_type=F32)
            o_ref[:, vs] = o
            upd = lax.dot_general(vh, k_out[:, ks].astype(BF16), tn, preferred_element_type=F32)
            state_ref[d, h] = st * dec[:, ks] + upd


def gla_scan(hm, a_lr, wdec, bdec, bsz):
    qo = OFF_GLA_Q // GLA_KEY_WIDTH
    ko = OFF_GLA_K // GLA_KEY_WIDTH
    vo = OFF_GLA_V // BRANCH_WIDTH

    def fwd(colblk):
        return lambda b, c: (b * N_CHUNKS + c, colblk)

    def bwd(colblk):
        return lambda b, c: (b * N_CHUNKS + N_CHUNKS - 1 - c, colblk)

    def specs(idx):
        return [pl.BlockSpec((GLA_CHUNK, GLA_KEY_WIDTH), idx(qo)),
                pl.BlockSpec((GLA_CHUNK, GLA_KEY_WIDTH), idx(ko)),
                pl.BlockSpec((GLA_CHUNK, BRANCH_WIDTH), idx(vo)),
                pl.BlockSpec((GLA_CHUNK, LANES), idx(0))]

    out = jax.ShapeDtypeStruct((bsz * SEQ, BRANCH_WIDTH), F32)
    return pl.pallas_call(
        _gla_scan_kernel,
        grid=(bsz, N_CHUNKS),
        in_specs=specs(fwd) + specs(bwd) + [
            pl.BlockSpec((2, LANES, GLA_KEY_WIDTH), lambda b, c: (0, 0, 0)),
            pl.BlockSpec((2, GLA_KEY_WIDTH), lambda b, c: (0, 0))],
        out_specs=[pl.BlockSpec((GLA_CHUNK, BRANCH_WIDTH), fwd(0)),
                   pl.BlockSpec((GLA_CHUNK, BRANCH_WIDTH), bwd(0))],
        out_shape=[out, out],
        scratch_shapes=[pltpu.VMEM((2, GLA_HEADS, GLA_VALUE_DIM, GLA_KEY_DIM), F32)],
        compiler_params=_cparams(("parallel", "arbitrary"), 32),
        name="gla_scan",
    )(hm, hm, hm, a_lr, hm, hm, hm, a_lr, wdec, bdec)


def _gla_out_kernel(of_ref, ob_ref, r_ref, g_ref, o_ref):
    for h in range(GLA_HEADS):
        vs = slice(h * GLA_VALUE_DIM, (h + 1) * GLA_VALUE_DIM)
        o = of_ref[:, vs] + ob_ref[:, vs]
        o = o * lax.rsqrt(jnp.mean(o * o, axis=-1, keepdims=True) + RMS_EPS) * g_ref[:, vs]
        r = r_ref[:, vs].astype(F32)
        o_ref[:, vs] = (o * (r * jax.nn.sigmoid(r))).astype(o_ref.dtype)


def gla_output(o_f, o_b, hm, norm_g, tm=512):
    m = o_f.shape[0]
    row = pl.BlockSpec((tm, BRANCH_WIDTH), lambda i: (i, 0))
    return pl.pallas_call(
        _gla_out_kernel,
        grid=(m // tm,),
        in_specs=[row, row,
                  pl.BlockSpec((tm, BRANCH_WIDTH), lambda i: (i, OFF_GLA_R // BRANCH_WIDTH)),
                  pl.BlockSpec((1, BRANCH_WIDTH), lambda i: (0, 0))],
        out_specs=row,
        out_shape=jax.ShapeDtypeStruct((m, BRANCH_WIDTH), BF16),
        compiler_params=_cparams(("parallel",), 40),
        name="gla_output",
    )(o_f, o_b, hm, norm_g.reshape(1, BRANCH_WIDTH))


def _mix_kernel(y0_ref, y1_ref, y2_ref, w_ref, g0_ref, g1_ref, g2_ref, o_ref):
    acc = None
    for i, (y_ref, g_ref) in enumerate(((y0_ref, g0_ref), (y1_ref, g1_ref), (y2_ref, g2_ref))):
        t = g_ref[...].astype(F32) * jnp.dot(y_ref[...], w_ref[i], preferred_element_type=F32)
        acc = t if acc is None else acc + t
    o_ref[...] = acc.astype(o_ref.dtype)


def gated_mix(ys, w_branch, layer, gates, tm=MM_TILE, tn=512):
    m = ys[0].shape[0]
    nj = D_MODEL // tn
    yspec = pl.BlockSpec((tm, BRANCH_WIDTH), lambda i, j: (i, 0))

    def gspec(br):
        return pl.BlockSpec((tm, tn), lambda i, j: (i, br * nj + j))

    return pl.pallas_call(
        _mix_kernel,
        grid=(m // tm, nj),
        in_specs=[yspec, yspec, yspec,
                  pl.BlockSpec((None, N_BRANCH, BRANCH_WIDTH, tn), lambda i, j: (layer, 0, 0, j)),
                  gspec(0), gspec(1), gspec(2)],
        out_specs=pl.BlockSpec((tm, tn), lambda i, j: (i, j)),
        out_shape=jax.ShapeDtypeStruct((m, D_MODEL), BF16),
        compiler_params=_cparams(("parallel", "parallel"), 52),
        name="gated_mix",
    )(*ys, w_branch, gates, gates, gates)


def _ln_kernel(*refs, alpha):
    *ins, g_ref, b_ref, of_ref, ob_ref = refs
    y = ins[-1][...]
    if len(ins) == 2:
        y = alpha * ins[0][...] + y
    mu = jnp.mean(y, axis=-1, keepdims=True)
    yc = y - mu
    var = jnp.mean(yc * yc, axis=-1, keepdims=True)
    out = yc * lax.rsqrt(var + LN_EPS) * g_ref[...] + b_ref[...]
    of_ref[...] = out
    ob_ref[...] = out.astype(BF16)


def residual_layer_norm(x, s, g, b, alpha, tm=256):
    m, d = s.shape
    row = pl.BlockSpec((tm, d), lambda i: (i, 0))
    vec = pl.BlockSpec((1, d), lambda i: (0, 0))
    acts = [s] if x is None else [x, s]
    return pl.pallas_call(
        functools.partial(_ln_kernel, alpha=alpha),
        grid=(m // tm,),
        in_specs=[row] * len(acts) + [vec, vec],
        out_specs=[row, row],
        out_shape=[jax.ShapeDtypeStruct((m, d), F32), jax.ShapeDtypeStruct((m, d), BF16)],
        compiler_params=_cparams(("parallel",), 40),
        name="residual_ln",
    )(*acts, g.reshape(1, d), b.reshape(1, d))


def kernel(x, w_in, b_gate, w_short_conv, na_rpb, gla_w_decay, gla_b_decay, gla_norm_g, w_branch, w_out,
           ln1_g, ln1_b, w_ffn_in, w_ffn_conv, b_ffn_conv, w_ffn_out, ln2_g, ln2_b):
    bsz, s, d = x.shape
    depth = w_in.shape[0]
    assert (s, d) == (SEQ, D_MODEL) and w_in.shape[2] == OFF_GATE + N_BRANCH * D_MODEL
    assert w_ffn_in.shape[2] == 2 * D_FF
    alpha = (2.0 * depth) ** 0.25
    ff_pad = D_FF_PAD - D_FF

    w_main_b, w_gate_b, w_dec_b = prep_w_in(jnp.swapaxes(w_in, 1, 2))
    w_ff_b, w_ffo_b = prep_w_ffn(w_ffn_in, w_ffn_out)
    w_br_b = w_branch.astype(BF16)
    w_out_b = w_out.astype(BF16)
    w_conv_ff = jnp.pad(w_ffn_conv, ((0, 0), (0, 0), (0, ff_pad)))
    b_conv_ff = jnp.pad(b_ffn_conv, ((0, 0), (0, ff_pad)))

    xf = x.reshape(bsz * s, d)
    xb = xf.astype(BF16)
    for l in range(depth):
        hm = matmul(xb, w_main_b, l, BF16, w_transposed=True)
        gates = matmul(xb, w_gate_b, l, BF16, bias=b_gate[l], w_transposed=True)
        a_lr = matmul(xb, w_dec_b, l, F32, w_transposed=True)

        y_conv = short_conv_mixer(hm, w_short_conv[l], bsz)
        y_na = neighbourhood_attention(hm, _na_pair_tables(na_rpb[l]), bsz)
        wdec = jnp.zeros((2, LANES, GLA_KEY_WIDTH), F32)
        wdec = wdec.at[0, :GLA_GATE_RANK].set(gla_w_decay[l, 0])
        wdec = wdec.at[1, GLA_GATE_RANK:2 * GLA_GATE_RANK].set(gla_w_decay[l, 1])
        o_f, o_b = gla_scan(hm, a_lr, wdec.astype(BF16), gla_b_decay[l], bsz)
        y_gla = gla_output(o_f, o_b, hm, gla_norm_g[l])

        mixed = gated_mix((y_conv, y_na, y_gla), w_br_b, l, gates)
        s1 = matmul(mixed, w_out_b, l, F32, residual=xf, alpha=alpha, tn=512)
        xf, xb = residual_layer_norm(None, s1, ln1_g[l], ln1_b[l], alpha)

        g = ffn_in(xb, w_ff_b, l, w_conv_ff[l], b_conv_ff[l].reshape(1, D_FF_PAD))
        s2 = matmul_kacc(g, w_ffo_b, l)
        xf, xb = residual_layer_norm(xf, s2, ln2_g[l], ln2_b[l], alpha)
    return xf.reshape(bsz, s, d)
```

```python
import functools

import numpy as np
import jax
import jax.numpy as jnp
from jax import lax
from jax.experimental import pallas as pl
from jax.experimental.pallas import tpu as pltpu

F32 = jnp.float32
BF16 = jnp.bfloat16

D_MODEL = 4096
SEQ = 4096
GRID_W = 64
ROWS = SEQ // GRID_W
BRANCH_WIDTH = D_MODEL // 2
N_BRANCH = 3
NA_HEAD_DIM = 128
NA_HEADS = BRANCH_WIDTH // NA_HEAD_DIM
NA_WIN_ROWS = 8
NA_WIN_COLS = 16
NA_QROWS = 8
NA_KROWS = 16
NA_HEADS_PER_STEP = 4
GLA_HEADS = 4
GLA_KEY_WIDTH = BRANCH_WIDTH // 2
GLA_KEY_DIM = GLA_KEY_WIDTH // GLA_HEADS
GLA_VALUE_DIM = BRANCH_WIDTH // GLA_HEADS
GLA_GATE_RANK = 16
GLA_GATE_TAU = 16.0
GLA_CHUNK = 64
N_CHUNKS = SEQ // GLA_CHUNK
D_FF = 256 * ((8 * D_MODEL // 3 + 255) // 256)
LN_EPS = 1e-5
RMS_EPS = 1e-6
NEG_INF = -1e30

OFF_CONV_B = 0
OFF_CONV_C = BRANCH_WIDTH
OFF_CONV_H = 2 * BRANCH_WIDTH
OFF_NA_Q = 3 * BRANCH_WIDTH
OFF_NA_K = 4 * BRANCH_WIDTH
OFF_NA_V = 5 * BRANCH_WIDTH
OFF_GLA_Q = 6 * BRANCH_WIDTH
OFF_GLA_K = OFF_GLA_Q + GLA_KEY_WIDTH
OFF_GLA_V = OFF_GLA_K + GLA_KEY_WIDTH
OFF_GLA_R = OFF_GLA_V + BRANCH_WIDTH
MAIN_WIDTH = OFF_GLA_R + BRANCH_WIDTH
OFF_GATE = MAIN_WIDTH + 2 * GLA_GATE_RANK

LANES = 128
V7X_VMEM_BYTES = 64 * 1024 * 1024
MM_TILE = 1024
D_FF_PAD = MM_TILE * (-(-D_FF // MM_TILE))
FFN_HALO = 16
MIB = 1024 * 1024


def _cparams(semantics, vmem_mib):
    assert vmem_mib * MIB < V7X_VMEM_BYTES
    return pltpu.CompilerParams(dimension_semantics=semantics, vmem_limit_bytes=vmem_mib * MIB)


def _cast_or_zero_kernel(src_ref, o_ref, *, axis, pad_index):
    is_pad = pl.program_id(axis) % (pad_index + 1) == pad_index

    @pl.when(is_pad)
    def _():
        o_ref[...] = jnp.zeros_like(o_ref)

    @pl.when(jnp.logical_not(is_pad))
    def _():
        o_ref[...] = src_ref[...].astype(o_ref.dtype)


def prep_w_ffn(w_ffn_in, w_ffn_out, tb=256):
    depth, d, _ = w_ffn_in.shape
    live, padded = D_FF // tb, D_FF_PAD // tb
    assert live * tb == D_FF and padded == live + 1
    w_in_b = pl.pallas_call(
        functools.partial(_cast_or_zero_kernel, axis=1, pad_index=live),
        grid=(depth, 2 * padded),
        in_specs=[pl.BlockSpec((None, d, tb),
                               lambda l, j: (l, 0, (j // padded) * live + jnp.minimum(j % padded, live - 1)))],
        out_specs=pl.BlockSpec((None, d, tb), lambda l, j: (l, 0, j)),
        out_shape=jax.ShapeDtypeStruct((depth, d, 2 * D_FF_PAD), BF16),
        compiler_params=_cparams(("parallel", "parallel"), 32), name="prep_w_ffn_in")(w_ffn_in)
    w_out_b = pl.pallas_call(
        functools.partial(_cast_or_zero_kernel, axis=1, pad_index=live),
        grid=(depth, padded),
        in_specs=[pl.BlockSpec((None, tb, d), lambda l, i: (l, jnp.minimum(i, live - 1), 0))],
        out_specs=pl.BlockSpec((None, tb, d), lambda l, i: (l, i, 0)),
        out_shape=jax.ShapeDtypeStruct((depth, D_FF_PAD, d), BF16),
        compiler_params=_cparams(("parallel", "parallel"), 32), name="prep_w_ffn_out")(w_ffn_out)
    return w_in_b, w_out_b


_NT = (((1,), (1,)), ((), ()))


def _mm_residual_kernel(a_ref, b_ref, res_ref, o_ref, *, alpha):
    acc = jnp.dot(a_ref[...], b_ref[...], preferred_element_type=F32)
    o_ref[...] = alpha * res_ref[...] + acc


def matmul_residual(a, w, layer, residual, alpha, tm=MM_TILE, tn=512):
    m, k = a.shape
    n = w.shape[2]
    assert n % tn == 0 and m % tm == 0
    return pl.pallas_call(
        functools.partial(_mm_residual_kernel, alpha=alpha),
        grid=(m // tm, n // tn),
        in_specs=[pl.BlockSpec((tm, k), lambda i, j: (i, 0)),
                  pl.BlockSpec((None, k, tn), lambda i, j: (layer, 0, j)),
                  pl.BlockSpec((tm, tn), lambda i, j: (i, j))],
        out_specs=pl.BlockSpec((tm, tn), lambda i, j: (i, j)),
        out_shape=jax.ShapeDtypeStruct((m, n), F32),
        compiler_params=_cparams(("parallel", "parallel"), 40),
        name="mm_residual",
    )(a, w, residual)


def _mm_wres_kernel(a_ref, w_ref, o_ref):
    acc = lax.dot_general(a_ref[...], w_ref[...].astype(BF16), _NT, preferred_element_type=F32)
    o_ref[...] = acc.astype(o_ref.dtype)


def _mm_wres_sigmoid_kernel(a_ref, w_ref, bias_ref, o_ref):
    acc = lax.dot_general(a_ref[...], w_ref[...].astype(BF16), _NT, preferred_element_type=F32)
    o_ref[...] = jax.nn.sigmoid(acc + bias_ref[...]).astype(o_ref.dtype)


def matmul_wres(a, w_t, layer, row0, n, out_dtype, bias=None, tm=MM_TILE, tn=768):
    m, k = a.shape
    assert n % tn == 0 and m % tm == 0 and row0 % 8 == 0
    in_specs = [pl.BlockSpec((tm, k), lambda j, i: (i, 0)),
                pl.BlockSpec((None, pl.Element(tn), pl.Element(k)),
                             lambda j, i: (layer, pl.multiple_of(row0 + j * tn, 8), 0))]
    args = [a, w_t]
    body = _mm_wres_kernel
    if bias is not None:
        in_specs.append(pl.BlockSpec((1, tn), lambda j, i: (0, j)))
        args.append(bias.reshape(1, n))
        body = _mm_wres_sigmoid_kernel
    return pl.pallas_call(
        body,
        grid=(n // tn, m // tm),
        in_specs=in_specs,
        out_specs=pl.BlockSpec((tm, tn), lambda j, i: (i, j)),
        out_shape=jax.ShapeDtypeStruct((m, n), out_dtype),
        compiler_params=_cparams(("parallel", "parallel"), 52),
        name="mm_wres" if bias is None else "mm_wres_sigmoid",
    )(*args)


def _mm_kacc_kernel(a_ref, b_ref, o_ref):
    @pl.when(pl.program_id(1) == 0)
    def _():
        o_ref[...] = jnp.zeros_like(o_ref)

    o_ref[...] += jnp.dot(a_ref[...], b_ref[...], preferred_element_type=F32)


def matmul_kacc(a, w, layer, tm=MM_TILE, tk=MM_TILE):
    m, k = a.shape
    n = w.shape[2]
    return pl.pallas_call(
        _mm_kacc_kernel,
        grid=(m // tm, k // tk),
        in_specs=[pl.BlockSpec((tm, tk), lambda i, kk: (i, kk)),
                  pl.BlockSpec((None, tk, n), lambda i, kk: (layer, kk, 0))],
        out_specs=pl.BlockSpec((tm, n), lambda i, kk: (i, 0)),
        out_shape=jax.ShapeDtypeStruct((m, n), F32),
        compiler_params=_cparams(("parallel", "arbitrary"), 56),
        name="mm_kacc",
    )(a, w)


def _conv3(p, w_ref):
    t = lax.broadcasted_iota(jnp.int32, p.shape, 0)
    prev = jnp.where(t == 0, 0.0, pltpu.roll(p, 1, 0))
    nxt = jnp.where(t == SEQ - 1, 0.0, pltpu.roll(p, SEQ - 1, 0))
    return prev * w_ref[0:1, :] + p * w_ref[1:2, :] + nxt * w_ref[2:3, :]


def _short_conv_kernel(b_ref, c_ref, h_ref, w_ref, o_ref):
    p = c_ref[...].astype(F32) * h_ref[...].astype(F32)
    o_ref[...] = (b_ref[...].astype(F32) * _conv3(p, w_ref)).astype(o_ref.dtype)


def short_conv_mixer(hm, w_conv, bsz, tc=256):
    nb = BRANCH_WIDTH // tc

    def spec(off):
        return pl.BlockSpec((SEQ, tc), lambda b, j: (b, off // tc + j))

    return pl.pallas_call(
        _short_conv_kernel,
        grid=(bsz, nb),
        in_specs=[spec(OFF_CONV_B), spec(OFF_CONV_C), spec(OFF_CONV_H),
                  pl.BlockSpec((3, tc), lambda b, j: (0, j))],
        out_specs=pl.BlockSpec((SEQ, tc), lambda b, j: (b, j)),
        out_shape=jax.ShapeDtypeStruct((bsz * SEQ, BRANCH_WIDTH), BF16),
        compiler_params=_cparams(("parallel", "parallel"), 48),
        name="short_conv",
    )(hm, hm, hm, w_conv)


def _ffn_in_kernel(xp_ref, x_ref, xn_ref, wa_ref, wu_ref, wc_ref, bc_ref, o_ref, xext_ref, *, tiles_per_seq):
    i = pl.program_id(0)
    tm = x_ref.shape[0]
    n_ext = tm + 2 * FFN_HALO

    @pl.when(pl.program_id(1) == 0)
    def _():
        xext_ref[0:FFN_HALO] = xp_ref[...]
        xext_ref[FFN_HALO:FFN_HALO + tm] = x_ref[...]
        xext_ref[FFN_HALO + tm:n_ext] = xn_ref[...]

    a_ext = jnp.dot(xext_ref[...], wa_ref[...], preferred_element_type=F32)
    u = jnp.dot(x_ref[...], wu_ref[...], preferred_element_type=F32)
    a = a_ext[FFN_HALO:FFN_HALO + tm]
    prev = pltpu.roll(a_ext, 1, 0)[FFN_HALO:FFN_HALO + tm]
    nxt = pltpu.roll(a_ext, n_ext - 1, 0)[FFN_HALO:FFN_HALO + tm]
    pos = lax.rem(i, tiles_per_seq)
    row = lax.broadcasted_iota(jnp.int32, a.shape, 0)
    prev = jnp.where(row == jnp.where(pos == 0, 0, -1), 0.0, prev)
    nxt = jnp.where(row == jnp.where(pos == tiles_per_seq - 1, tm - 1, -1), 0.0, nxt)
    c = prev * wc_ref[0:1, :] + a * wc_ref[1:2, :] + nxt * wc_ref[2:3, :] + bc_ref[...]
    g = 0.5 * c * (1.0 + lax.erf(c * (2.0 ** -0.5)))
    o_ref[...] = (g * u).astype(o_ref.dtype)


def ffn_in(xb, w_ff, layer, w_conv, b_conv, tm=MM_TILE, tn=512):
    m, k = xb.shape
    nf = w_ff.shape[2] // 2
    nb = nf // tn
    hb = tm // FFN_HALO
    return pl.pallas_call(
        functools.partial(_ffn_in_kernel, tiles_per_seq=SEQ // tm),
        grid=(m // tm, nb),
        in_specs=[pl.BlockSpec((FFN_HALO, k), lambda i, j: (jnp.maximum(i * hb - 1, 0), 0)),
                  pl.BlockSpec((tm, k), lambda i, j: (i, 0)),
                  pl.BlockSpec((FFN_HALO, k), lambda i, j: (jnp.minimum((i + 1) * hb, m // FFN_HALO - 1), 0)),
                  pl.BlockSpec((None, k, tn), lambda i, j: (layer, 0, j)),
                  pl.BlockSpec((None, k, tn), lambda i, j: (layer, 0, nb + j)),
                  pl.BlockSpec((3, tn), lambda i, j: (0, j)),
                  pl.BlockSpec((1, tn), lambda i, j: (0, j))],
        out_specs=pl.BlockSpec((tm, tn), lambda i, j: (i, j)),
        out_shape=jax.ShapeDtypeStruct((m, nf), BF16),
        scratch_shapes=[pltpu.VMEM((tm + 2 * FFN_HALO, k), BF16)],
        compiler_params=_cparams(("parallel", "arbitrary"), 52),
        name="ffn_in",
    )(xb, xb, xb, w_ff, w_ff, w_conv, b_conv)


def _na_key_start(band):
    lo = band * NA_QROWS - NA_WIN_ROWS // 2
    hi = ROWS - NA_KROWS
    if isinstance(band, int):
        return min(max(lo, 0), hi)
    return jnp.clip(lo, 0, hi)


NA_ROW_OFFSETS = 2 * NA_WIN_ROWS - 1
NA_PAIR_CODES = 3 * NA_ROW_OFFSETS


def _na_pair_tables(rpb):
    qc = np.arange(GRID_W)[:, None]
    kc = np.arange(GRID_W)[None, :]
    wc = np.clip(qc - NA_WIN_COLS // 2, 0, GRID_W - NA_WIN_COLS)
    col_valid = (kc >= wc) & (kc < wc + NA_WIN_COLS)
    cidx = np.clip(kc - qc + NA_WIN_COLS - 1, 0, 2 * NA_WIN_COLS - 2)
    tm = jnp.where(col_valid[None, None], rpb[:, :, cidx], NEG_INF)
    tm = jnp.concatenate([tm, jnp.full_like(tm[:, :1], NEG_INF)], axis=1)
    out = NA_ROW_OFFSETS
    both = np.arange(out - 1)
    one = np.arange(out)
    left = np.concatenate([both, one, np.full(out, out), [out]])
    right = np.concatenate([both + 1, np.full(out, out), one, [out]])
    return jnp.concatenate([tm[:, left], tm[:, right]], axis=-1)


def _na_pair_code(band, i, jp):
    r = band * NA_QROWS + i
    wr = jnp.clip(r - NA_WIN_ROWS // 2, 0, ROWS - NA_WIN_ROWS)
    kl = _na_key_start(band) + 2 * jp
    dl = kl - r + NA_WIN_ROWS - 1
    vl = jnp.logical_and(kl >= wr, kl < wr + NA_WIN_ROWS)
    vr = jnp.logical_and(kl + 1 >= wr, kl + 1 < wr + NA_WIN_ROWS)
    out = NA_ROW_OFFSETS
    return jnp.where(jnp.logical_and(vl, vr), dl,
                     jnp.where(vl, out - 1 + dl, jnp.where(vr, 2 * out - 1 + dl + 1, 3 * out - 1)))


def _na_kernel(q_ref, k_ref, v_ref, bias_ref, o_ref):
    band = pl.program_id(2)
    k0 = pl.multiple_of(_na_key_start(band) * GRID_W, GRID_W)
    nk = NA_KROWS * GRID_W
    codes = [[_na_pair_code(band, i, jp) for jp in range(NA_KROWS // 2)] for i in range(NA_QROWS)]
    for h in range(NA_HEADS_PER_STEP):
        cs = slice(h * NA_HEAD_DIM, (h + 1) * NA_HEAD_DIM)
        k = k_ref[pl.ds(k0, nk), cs]
        v = v_ref[pl.ds(k0, nk), cs]
        bias = jnp.concatenate([jnp.concatenate([bias_ref[h, c] for c in row], axis=1) for row in codes], axis=0)
        s = lax.dot_general(q_ref[:, cs], k, (((1,), (1,)), ((), ())), preferred_element_type=F32)
        s = s * (NA_HEAD_DIM ** -0.5) + bias
        m = jnp.max(s, axis=-1, keepdims=True)
        p = jnp.exp(s - m)
        l = jnp.sum(p, axis=-1, keepdims=True)
        o = jnp.dot(p.astype(BF16), v, preferred_element_type=F32)
        o_ref[:, cs] = (o / l).astype(o_ref.dtype)


def neighbourhood_attention(hm, bias_tab, bsz):
    nbands = ROWS // NA_QROWS
    tq = NA_QROWS * GRID_W
    hp = NA_HEADS_PER_STEP
    wg = hp * NA_HEAD_DIM
    qo, ko, vo = (off // wg for off in (OFF_NA_Q, OFF_NA_K, OFF_NA_V))
    return pl.pallas_call(
        _na_kernel,
        grid=(NA_HEADS // hp, bsz, nbands),
        in_specs=[
            pl.BlockSpec((tq, wg), lambda g, b, r: (b * nbands + r, qo + g)),
            pl.BlockSpec((SEQ, wg), lambda g, b, r: (b, ko + g)),
            pl.BlockSpec((SEQ, wg), lambda g, b, r: (b, vo + g)),
            pl.BlockSpec((hp, NA_PAIR_CODES, GRID_W, 2 * GRID_W), lambda g, b, r: (g, 0, 0, 0)),
        ],
        out_specs=pl.BlockSpec((tq, wg), lambda g, b, r: (b * nbands + r, g)),
        out_shape=jax.ShapeDtypeStruct((bsz * SEQ, BRANCH_WIDTH), BF16),
        compiler_params=_cparams(("parallel", "parallel", "parallel"), 48),
        name="natten",
    )(hm, hm, hm, bias_tab)


def _gla_scan_kernel(qf_ref, kf_ref, vf_ref, af_ref, qb_ref, kb_ref, vb_ref, ab_ref,
                     wdec_ref, bdec_ref, of_ref, ob_ref, state_ref):
    @pl.when(pl.program_id(1) == 0)
    def _():
        state_ref[...] = jnp.zeros_like(state_ref)

    row = lax.broadcasted_iota(jnp.int32, (GLA_CHUNK, GLA_CHUNK), 0)
    col = lax.broadcasted_iota(jnp.int32, (GLA_CHUNK, GLA_CHUNK), 1)
    nt = (((1,), (1,)), ((), ()))
    tn = (((0,), (0,)), ((), ()))
    dirs = ((qf_ref, kf_ref, vf_ref, af_ref, of_ref, row >= col, GLA_CHUNK - 1),
            (qb_ref, kb_ref, vb_ref, ab_ref, ob_ref, row <= col, 0))
    for d, (q_ref, k_ref, v_ref, a_ref, o_ref, tri, last_row) in enumerate(dirs):
        z = jnp.dot(a_ref[...].astype(BF16), wdec_ref[d], preferred_element_type=F32) + bdec_ref[d:d + 1, :]
        log_a = (jnp.minimum(z, 0.0) - jnp.log1p(jnp.exp(-jnp.abs(z)))) * (1.0 / GLA_GATE_TAU)
        cum = jnp.dot(tri.astype(F32), log_a, precision=lax.Precision.HIGHEST, preferred_element_type=F32)
        last = cum[last_row:last_row + 1, :]
        q_in = q_ref[...].astype(F32) * ((GLA_KEY_DIM ** -0.5) * jnp.exp(cum))
        kk = k_ref[...].astype(F32)
        k_in = kk * jnp.exp(-cum)
        k_out = kk * jnp.exp(last - cum)
        dec = jnp.exp(last)
        for h in range(GLA_HEADS):
            ks = slice(h * GLA_KEY_DIM, (h + 1) * GLA_KEY_DIM)
            vs = slice(h * GLA_VALUE_DIM, (h + 1) * GLA_VALUE_DIM)
            qh = q_in[:, ks].astype(BF16)
            vh = v_ref[:, vs]
            att = lax.dot_general(qh, k_in[:, ks].astype(BF16), nt, preferred_element_type=F32)
            att = jnp.where(tri, att, 0.0).astype(BF16)
            st = state_ref[d, h]
            o = jnp.dot(att, vh, preferred_element_type=F32)
            o = o + lax.dot_general(qh, st.astype(BF16), nt, preferred_element_type=F32)
            o_ref[:, vs] = o
            upd = lax.dot_general(vh, k_out[:, ks].astype(BF16), tn, preferred_element_type=F32)
            state_ref[d, h] = st * dec[:, ks] + upd


def gla_scan(hm, a_lr, wdec, bdec, bsz):
    qo = OFF_GLA_Q // GLA_KEY_WIDTH
    ko = OFF_GLA_K // GLA_KEY_WIDTH
    vo = OFF_GLA_V // BRANCH_WIDTH

    def fwd(colblk):
        return lambda b, c: (b * N_CHUNKS + c, colblk)

    def bwd(colblk):
        return lambda b, c: (b * N_CHUNKS + N_CHUNKS - 1 - c, colblk)

    def specs(idx):
        return [pl.BlockSpec((GLA_CHUNK, GLA_KEY_WIDTH), idx(qo)),
                pl.BlockSpec((GLA_CHUNK, GLA_KEY_WIDTH), idx(ko)),
                pl.BlockSpec((GLA_CHUNK, BRANCH_WIDTH), idx(vo)),
                pl.BlockSpec((GLA_CHUNK, LANES), idx(0))]

    out = jax.ShapeDtypeStruct((bsz * SEQ, BRANCH_WIDTH), F32)
    return pl.pallas_call(
        _gla_scan_kernel,
        grid=(bsz, N_CHUNKS),
        in_specs=specs(fwd) + specs(bwd) + [
            pl.BlockSpec((2, LANES, GLA_KEY_WIDTH), lambda b, c: (0, 0, 0)),
            pl.BlockSpec((2, GLA_KEY_WIDTH), lambda b, c: (0, 0))],
        out_specs=[pl.BlockSpec((GLA_CHUNK, BRANCH_WIDTH), fwd(0)),
                   pl.BlockSpec((GLA_CHUNK, BRANCH_WIDTH), bwd(0))],
        out_shape=[out, out],
        scratch_shapes=[pltpu.VMEM((2, GLA_HEADS, GLA_VALUE_DIM, GLA_KEY_DIM), F32)],
        compiler_params=_cparams(("parallel", "arbitrary"), 32),
        name="gla_scan",
    )(hm, hm, hm, a_lr, hm, hm, hm, a_lr, wdec, bdec)


def _gla_out_kernel(of_ref, ob_ref, r_ref, g_ref, o_ref):
    for h in range(GLA_HEADS):
        vs = slice(h * GLA_VALUE_DIM, (h + 1) * GLA_VALUE_DIM)
        o = of_ref[:, vs] + ob_ref[:, vs]
        o = o * lax.rsqrt(jnp.mean(o * o, axis=-1, keepdims=True) + RMS_EPS) * g_ref[:, vs]
        r = r_ref[:, vs].astype(F32)
        o_ref[:, vs] = (o * (r * jax.nn.sigmoid(r))).astype(o_ref.dtype)


def gla_output(o_f, o_b, hm, norm_g, tm=512):
    m = o_f.shape[0]
    row = pl.BlockSpec((tm, BRANCH_WIDTH), lambda i: (i, 0))
    return pl.pallas_call(
        _gla_out_kernel,
        grid=(m // tm,),
        in_specs=[row, row,
                  pl.BlockSpec((tm, BRANCH_WIDTH), lambda i: (i, OFF_GLA_R // BRANCH_WIDTH)),
                  pl.BlockSpec((1, BRANCH_WIDTH), lambda i: (0, 0))],
        out_specs=row,
        out_shape=jax.ShapeDtypeStruct((m, BRANCH_WIDTH), BF16),
        compiler_params=_cparams(("parallel",), 40),
        name="gla_output",
    )(o_f, o_b, hm, norm_g.reshape(1, BRANCH_WIDTH))


def _mix_kernel(y0_ref, y1_ref, y2_ref, w_ref, g0_ref, g1_ref, g2_ref, o_ref):
    acc = None
    for i, (y_ref, g_ref) in enumerate(((y0_ref, g0_ref), (y1_ref, g1_ref), (y2_ref, g2_ref))):
        t = g_ref[...].astype(F32) * jnp.dot(y_ref[...], w_ref[i], preferred_element_type=F32)
        acc = t if acc is None else acc + t
    o_ref[...] = acc.astype(o_ref.dtype)


def gated_mix(ys, w_branch, layer, gates, tm=MM_TILE, tn=512):
    m = ys[0].shape[0]
    nj = D_MODEL // tn
    yspec = pl.BlockSpec((tm, BRANCH_WIDTH), lambda i, j: (i, 0))

    def gspec(br):
        return pl.BlockSpec((tm, tn), lambda i, j: (i, br * nj + j))

    return pl.pallas_call(
        _mix_kernel,
        grid=(m // tm, nj),
        in_specs=[yspec, yspec, yspec,
                  pl.BlockSpec((None, N_BRANCH, BRANCH_WIDTH, tn), lambda i, j: (layer, 0, 0, j)),
                  gspec(0), gspec(1), gspec(2)],
        out_specs=pl.BlockSpec((tm, tn), lambda i, j: (i, j)),
        out_shape=jax.ShapeDtypeStruct((m, D_MODEL), BF16),
        compiler_params=_cparams(("parallel", "parallel"), 52),
        name="gated_mix",
    )(*ys, w_branch, gates, gates, gates)


def _ln_kernel(*refs, alpha):
    *ins, g_ref, b_ref, of_ref, ob_ref = refs
    y = ins[-1][...]
    if len(ins) == 2:
        y = alpha * ins[0][...] + y
    mu = jnp.mean(y, axis=-1, keepdims=True)
    yc = y - mu
    var = jnp.mean(yc * yc, axis=-1, keepdims=True)
    out = yc * lax.rsqrt(var + LN_EPS) * g_ref[...] + b_ref[...]
    of_ref[...] = out
    ob_ref[...] = out.astype(BF16)


def residual_layer_norm(x, s, g, b, alpha, tm=256):
    m, d = s.shape
    row = pl.BlockSpec((tm, d), lambda i: (i, 0))
    vec = pl.BlockSpec((1, d), lambda i: (0, 0))
    acts = [s] if x is None else [x, s]
    return pl.pallas_call(
        functools.partial(_ln_kernel, alpha=alpha),
        grid=(m // tm,),
        in_specs=[row] * len(acts) + [vec, vec],
        out_specs=[row, row],
        out_shape=[jax.ShapeDtypeStruct((m, d), F32), jax.ShapeDtypeStruct((m, d), BF16)],
        compiler_params=_cparams(("parallel",), 40),
        name="residual_ln",
    )(*acts, g.reshape(1, d), b.reshape(1, d))


def kernel(x, w_in, b_gate, w_short_conv, na_rpb, gla_w_decay, gla_b_decay, gla_norm_g, w_branch, w_out,
           ln1_g, ln1_b, w_ffn_in, w_ffn_conv, b_ffn_conv, w_ffn_out, ln2_g, ln2_b):
    bsz, s, d = x.shape
    depth = w_in.shape[0]
    assert (s, d) == (SEQ, D_MODEL) and w_in.shape[2] == OFF_GATE + N_BRANCH * D_MODEL
    assert w_ffn_in.shape[2] == 2 * D_FF
    alpha = (2.0 * depth) ** 0.25
    ff_pad = D_FF_PAD - D_FF

    w_in_t = jnp.swapaxes(w_in, 1, 2)
    w_ff_b, w_ffo_b = prep_w_ffn(w_ffn_in, w_ffn_out)
    w_br_b = w_branch.astype(BF16)
    w_out_b = w_out.astype(BF16)
    w_conv_ff = jnp.pad(w_ffn_conv, ((0, 0), (0, 0), (0, ff_pad)))
    b_conv_ff = jnp.pad(b_ffn_conv, ((0, 0), (0, ff_pad)))

    xf = x.reshape(bsz * s, d)
    xb = xf.astype(BF16)
    for l in range(depth):
        hm = matmul_wres(xb, w_in_t, l, 0, MAIN_WIDTH, BF16)
        gates = matmul_wres(xb, w_in_t, l, OFF_GATE, N_BRANCH * D_MODEL, BF16, bias=b_gate[l])
        a_lr = matmul_wres(xb, w_in_t, l, MAIN_WIDTH, LANES, F32, tn=LANES)

        y_conv = short_conv_mixer(hm, w_short_conv[l], bsz)
        y_na = neighbourhood_attention(hm, _na_pair_tables(na_rpb[l]), bsz)
        wdec = jnp.zeros((2, LANES, GLA_KEY_WIDTH), F32)
        wdec = wdec.at[0, :GLA_GATE_RANK].set(gla_w_decay[l, 0])
        wdec = wdec.at[1, GLA_GATE_RANK:2 * GLA_GATE_RANK].set(gla_w_decay[l, 1])
        o_f, o_b = gla_scan(hm, a_lr, wdec.astype(BF16), gla_b_decay[l], bsz)
        y_gla = gla_output(o_f, o_b, hm, gla_norm_g[l])

        mixed = gated_mix((y_conv, y_na, y_gla), w_br_b, l, gates)
        s1 = matmul_residual(mixed, w_out_b, l, xf, alpha)
        xf, xb = residual_layer_norm(None, s1, ln1_g[l], ln1_b[l], alpha)

        g = ffn_in(xb, w_ff_b, l, w_conv_ff[l], b_conv_ff[l].reshape(1, D_FF_PAD))
        s2 = matmul_kacc(g, w_ffo_b, l)
        xf, xb = residual_layer_norm(xf, s2, ln2_g[l], ln2_b[l], alpha)
    return xf.reshape(bsz, s, d)
```

```python
import functools

import numpy as np
import jax
import jax.numpy as jnp
from jax import lax
from jax.experimental import pallas as pl
from jax.experimental.pallas import tpu as pltpu

F32 = jnp.float32
BF16 = jnp.bfloat16

D_MODEL = 4096
SEQ = 4096
GRID_W = 64
ROWS = SEQ // GRID_W
BRANCH_WIDTH = D_MODEL // 2
N_BRANCH = 3
NA_HEAD_DIM = 128
NA_HEADS = BRANCH_WIDTH // NA_HEAD_DIM
NA_WIN_ROWS = 8
NA_WIN_COLS = 16
NA_QROWS = 8
NA_KROWS = 16
NA_HEADS_PER_STEP = 4
GLA_HEADS = 4
GLA_KEY_WIDTH = BRANCH_WIDTH // 2
GLA_KEY_DIM = GLA_KEY_WIDTH // GLA_HEADS
GLA_VALUE_DIM = BRANCH_WIDTH // GLA_HEADS
GLA_GATE_RANK = 16
GLA_GATE_TAU = 16.0
GLA_CHUNK = 64
N_CHUNKS = SEQ // GLA_CHUNK
D_FF = 256 * ((8 * D_MODEL // 3 + 255) // 256)
LN_EPS = 1e-5
RMS_EPS = 1e-6
NEG_INF = -1e30

OFF_CONV_B = 0
OFF_CONV_C = BRANCH_WIDTH
OFF_CONV_H = 2 * BRANCH_WIDTH
OFF_NA_Q = 3 * BRANCH_WIDTH
OFF_NA_K = 4 * BRANCH_WIDTH
OFF_NA_V = 5 * BRANCH_WIDTH
OFF_GLA_Q = 6 * BRANCH_WIDTH
OFF_GLA_K = OFF_GLA_Q + GLA_KEY_WIDTH
OFF_GLA_V = OFF_GLA_K + GLA_KEY_WIDTH
OFF_GLA_R = OFF_GLA_V + BRANCH_WIDTH
MAIN_WIDTH = OFF_GLA_R + BRANCH_WIDTH
OFF_GATE = MAIN_WIDTH + 2 * GLA_GATE_RANK

LANES = 128
V7X_VMEM_BYTES = 64 * 1024 * 1024
MM_TILE = 1024
D_FF_PAD = MM_TILE * (-(-D_FF // MM_TILE))
FFN_HALO = 16
MIB = 1024 * 1024


def _cparams(semantics, vmem_mib):
    assert vmem_mib * MIB < V7X_VMEM_BYTES
    return pltpu.CompilerParams(dimension_semantics=semantics, vmem_limit_bytes=vmem_mib * MIB)


def _cast_or_zero_kernel(src_ref, o_ref, *, axis, pad_index):
    is_pad = pl.program_id(axis) % (pad_index + 1) == pad_index

    @pl.when(is_pad)
    def _():
        o_ref[...] = jnp.zeros_like(o_ref)

    @pl.when(jnp.logical_not(is_pad))
    def _():
        o_ref[...] = src_ref[...].astype(o_ref.dtype)


def prep_w_ffn(w_ffn_in, w_ffn_out, tb=256):
    depth, d, _ = w_ffn_in.shape
    live, padded = D_FF // tb, D_FF_PAD // tb
    assert live * tb == D_FF and padded == live + 1
    w_in_b = pl.pallas_call(
        functools.partial(_cast_or_zero_kernel, axis=1, pad_index=live),
        grid=(depth, 2 * padded),
        in_specs=[pl.BlockSpec((None, d, tb),
                               lambda l, j: (l, 0, (j // padded) * live + jnp.minimum(j % padded, live - 1)))],
        out_specs=pl.BlockSpec((None, d, tb), lambda l, j: (l, 0, j)),
        out_shape=jax.ShapeDtypeStruct((depth, d, 2 * D_FF_PAD), BF16),
        compiler_params=_cparams(("parallel", "parallel"), 32), name="prep_w_ffn_in")(w_ffn_in)
    w_out_b = pl.pallas_call(
        functools.partial(_cast_or_zero_kernel, axis=1, pad_index=live),
        grid=(depth, padded),
        in_specs=[pl.BlockSpec((None, tb, d), lambda l, i: (l, jnp.minimum(i, live - 1), 0))],
        out_specs=pl.BlockSpec((None, tb, d), lambda l, i: (l, i, 0)),
        out_shape=jax.ShapeDtypeStruct((depth, D_FF_PAD, d), BF16),
        compiler_params=_cparams(("parallel", "parallel"), 32), name="prep_w_ffn_out")(w_ffn_out)
    return w_in_b, w_out_b


_NT = (((1,), (1,)), ((), ()))


def _mm_residual_kernel(a_ref, b_ref, res_ref, o_ref, *, alpha):
    acc = jnp.dot(a_ref[...], b_ref[...], preferred_element_type=F32)
    o_ref[...] = alpha * res_ref[...] + acc


def matmul_residual(a, w, layer, residual, alpha, tm=MM_TILE, tn=512):
    m, k = a.shape
    n = w.shape[2]
    assert n % tn == 0 and m % tm == 0
    return pl.pallas_call(
        functools.partial(_mm_residual_kernel, alpha=alpha),
        grid=(m // tm, n // tn),
        in_specs=[pl.BlockSpec((tm, k), lambda i, j: (i, 0)),
                  pl.BlockSpec((None, k, tn), lambda i, j: (layer, 0, j)),
                  pl.BlockSpec((tm, tn), lambda i, j: (i, j))],
        out_specs=pl.BlockSpec((tm, tn), lambda i, j: (i, j)),
        out_shape=jax.ShapeDtypeStruct((m, n), F32),
        compiler_params=_cparams(("parallel", "parallel"), 40),
        name="mm_residual",
    )(a, w, residual)


def _mm_wres_kernel(a_ref, w_ref, o_ref):
    acc = lax.dot_general(a_ref[...], w_ref[...].astype(BF16), _NT, preferred_element_type=F32)
    o_ref[...] = acc.astype(o_ref.dtype)


def _mm_wres_sigmoid_kernel(a_ref, w_ref, bias_ref, o_ref):
    acc = lax.dot_general(a_ref[...], w_ref[...].astype(BF16), _NT, preferred_element_type=F32)
    o_ref[...] = jax.nn.sigmoid(acc + bias_ref[...]).astype(o_ref.dtype)


def matmul_wres(a, w_t, layer, row0, n, out_dtype, bias=None, tm=MM_TILE, tn=768):
    m, k = a.shape
    assert n % tn == 0 and m % tm == 0 and row0 % 8 == 0
    in_specs = [pl.BlockSpec((tm, k), lambda j, i: (i, 0)),
                pl.BlockSpec((None, pl.Element(tn), pl.Element(k)),
                             lambda j, i: (layer, pl.multiple_of(row0 + j * tn, 8), 0))]
    args = [a, w_t]
    body = _mm_wres_kernel
    if bias is not None:
        in_specs.append(pl.BlockSpec((1, tn), lambda j, i: (0, j)))
        args.append(bias.reshape(1, n))
        body = _mm_wres_sigmoid_kernel
    return pl.pallas_call(
        body,
        grid=(n // tn, m // tm),
        in_specs=in_specs,
        out_specs=pl.BlockSpec((tm, tn), lambda j, i: (i, j)),
        out_shape=jax.ShapeDtypeStruct((m, n), out_dtype),
        compiler_params=_cparams(("parallel", "parallel"), 52),
        name="mm_wres" if bias is None else "mm_wres_sigmoid",
    )(*args)


def _mm_kacc_kernel(a_ref, b_ref, o_ref, *, k_last):
    kk = pl.program_id(1)
    last = pl.num_programs(1) - 1

    @pl.when(kk == 0)
    def _():
        o_ref[...] = jnp.zeros_like(o_ref)

    @pl.when(kk < last)
    def _():
        o_ref[...] += jnp.dot(a_ref[...], b_ref[...], preferred_element_type=F32)

    @pl.when(kk == last)
    def _():
        o_ref[...] += jnp.dot(a_ref[:, :k_last], b_ref[:k_last, :], preferred_element_type=F32)


def matmul_kacc(a, w, layer, k_live, tm=MM_TILE, tk=MM_TILE):
    m, k = a.shape
    n = w.shape[2]
    assert k % tk == 0 and 0 < k - k_live < tk
    return pl.pallas_call(
        functools.partial(_mm_kacc_kernel, k_last=k_live - (k - tk)),
        grid=(m // tm, k // tk),
        in_specs=[pl.BlockSpec((tm, tk), lambda i, kk: (i, kk)),
                  pl.BlockSpec((None, tk, n), lambda i, kk: (layer, kk, 0))],
        out_specs=pl.BlockSpec((tm, n), lambda i, kk: (i, 0)),
        out_shape=jax.ShapeDtypeStruct((m, n), F32),
        compiler_params=_cparams(("parallel", "arbitrary"), 56),
        name="mm_kacc",
    )(a, w)


def _conv3(p, w_ref):
    t = lax.broadcasted_iota(jnp.int32, p.shape, 0)
    prev = jnp.where(t == 0, 0.0, pltpu.roll(p, 1, 0))
    nxt = jnp.where(t == SEQ - 1, 0.0, pltpu.roll(p, SEQ - 1, 0))
    return prev * w_ref[0:1, :] + p * w_ref[1:2, :] + nxt * w_ref[2:3, :]


def _short_conv_kernel(b_ref, c_ref, h_ref, w_ref, o_ref):
    p = c_ref[...].astype(F32) * h_ref[...].astype(F32)
    o_ref[...] = (b_ref[...].astype(F32) * _conv3(p, w_ref)).astype(o_ref.dtype)


def short_conv_mixer(hm, w_conv, bsz, tc=256):
    nb = BRANCH_WIDTH // tc

    def spec(off):
        return pl.BlockSpec((SEQ, tc), lambda b, j: (b, off // tc + j))

    return pl.pallas_call(
        _short_conv_kernel,
        grid=(bsz, nb),
        in_specs=[spec(OFF_CONV_B), spec(OFF_CONV_C), spec(OFF_CONV_H),
                  pl.BlockSpec((3, tc), lambda b, j: (0, j))],
        out_specs=pl.BlockSpec((SEQ, tc), lambda b, j: (b, j)),
        out_shape=jax.ShapeDtypeStruct((bsz * SEQ, BRANCH_WIDTH), BF16),
        compiler_params=_cparams(("parallel", "parallel"), 48),
        name="short_conv",
    )(hm, hm, hm, w_conv)


def _ffn_in_kernel(xp_ref, x_ref, xn_ref, wa_ref, wu_ref, wc_ref, bc_ref, o_ref, xext_ref, *,
                   tiles_per_seq, n_last):
    i = pl.program_id(0)
    tm = x_ref.shape[0]
    n_ext = tm + 2 * FFN_HALO

    @pl.when(pl.program_id(1) == 0)
    def _():
        xext_ref[0:FFN_HALO] = xp_ref[...]
        xext_ref[FFN_HALO:FFN_HALO + tm] = x_ref[...]
        xext_ref[FFN_HALO + tm:n_ext] = xn_ref[...]

    def compute(ncols):
        cs = slice(0, ncols)
        a_ext = jnp.dot(xext_ref[...], wa_ref[:, cs], preferred_element_type=F32)
        u = jnp.dot(x_ref[...], wu_ref[:, cs], preferred_element_type=F32)
        a = a_ext[FFN_HALO:FFN_HALO + tm]
        prev = pltpu.roll(a_ext, 1, 0)[FFN_HALO:FFN_HALO + tm]
        nxt = pltpu.roll(a_ext, n_ext - 1, 0)[FFN_HALO:FFN_HALO + tm]
        pos = lax.rem(i, tiles_per_seq)
        row = lax.broadcasted_iota(jnp.int32, a.shape, 0)
        prev = jnp.where(row == jnp.where(pos == 0, 0, -1), 0.0, prev)
        nxt = jnp.where(row == jnp.where(pos == tiles_per_seq - 1, tm - 1, -1), 0.0, nxt)
        c = prev * wc_ref[0:1, cs] + a * wc_ref[1:2, cs] + nxt * wc_ref[2:3, cs] + bc_ref[:, cs]
        g = 0.5 * c * (1.0 + lax.erf(c * (2.0 ** -0.5)))
        o_ref[:, cs] = (g * u).astype(o_ref.dtype)
        if ncols < o_ref.shape[1]:
            o_ref[:, ncols:] = jnp.zeros((tm, o_ref.shape[1] - ncols), o_ref.dtype)

    j = pl.program_id(1)
    last = pl.num_programs(1) - 1
    pl.when(j < last)(lambda: compute(o_ref.shape[1]))
    pl.when(j == last)(lambda: compute(n_last))


def ffn_in(xb, w_ff, layer, w_conv, b_conv, n_live, tm=MM_TILE, tn=512):
    m, k = xb.shape
    nf = w_ff.shape[2] // 2
    nb = nf // tn
    hb = tm // FFN_HALO
    assert nf % tn == 0 and 0 < nf - n_live < tn
    return pl.pallas_call(
        functools.partial(_ffn_in_kernel, tiles_per_seq=SEQ // tm, n_last=n_live - (nf - tn)),
        grid=(m // tm, nb),
        in_specs=[pl.BlockSpec((FFN_HALO, k), lambda i, j: (jnp.maximum(i * hb - 1, 0), 0)),
                  pl.BlockSpec((tm, k), lambda i, j: (i, 0)),
                  pl.BlockSpec((FFN_HALO, k), lambda i, j: (jnp.minimum((i + 1) * hb, m // FFN_HALO - 1), 0)),
                  pl.BlockSpec((None, k, tn), lambda i, j: (layer, 0, j)),
                  pl.BlockSpec((None, k, tn), lambda i, j: (layer, 0, nb + j)),
                  pl.BlockSpec((3, tn), lambda i, j: (0, j)),
                  pl.BlockSpec((1, tn), lambda i, j: (0, j))],
        out_specs=pl.BlockSpec((tm, tn), lambda i, j: (i, j)),
        out_shape=jax.ShapeDtypeStruct((m, nf), BF16),
        scratch_shapes=[pltpu.VMEM((tm + 2 * FFN_HALO, k), BF16)],
        compiler_params=_cparams(("parallel", "arbitrary"), 52),
        name="ffn_in",
    )(xb, xb, xb, w_ff, w_ff, w_conv, b_conv)


def _na_key_start(band):
    lo = band * NA_QROWS - NA_WIN_ROWS // 2
    hi = ROWS - NA_KROWS
    if isinstance(band, int):
        return min(max(lo, 0), hi)
    return jnp.clip(lo, 0, hi)


NA_ROW_OFFSETS = 2 * NA_WIN_ROWS - 1
NA_PAIR_CODES = 3 * NA_ROW_OFFSETS


def _na_pair_tables(rpb):
    qc = np.arange(GRID_W)[:, None]
    kc = np.arange(GRID_W)[None, :]
    wc = np.clip(qc - NA_WIN_COLS // 2, 0, GRID_W - NA_WIN_COLS)
    col_valid = (kc >= wc) & (kc < wc + NA_WIN_COLS)
    cidx = np.clip(kc - qc + NA_WIN_COLS - 1, 0, 2 * NA_WIN_COLS - 2)
    tm = jnp.where(col_valid[None, None], rpb[:, :, cidx], NEG_INF)
    tm = jnp.concatenate([tm, jnp.full_like(tm[:, :1], NEG_INF)], axis=1)
    out = NA_ROW_OFFSETS
    both = np.arange(out - 1)
    one = np.arange(out)
    left = np.concatenate([both, one, np.full(out, out), [out]])
    right = np.concatenate([both + 1, np.full(out, out), one, [out]])
    return jnp.concatenate([tm[:, left], tm[:, right]], axis=-1)


def _na_pair_code(band, i, jp):
    r = band * NA_QROWS + i
    wr = jnp.clip(r - NA_WIN_ROWS // 2, 0, ROWS - NA_WIN_ROWS)
    kl = _na_key_start(band) + 2 * jp
    dl = kl - r + NA_WIN_ROWS - 1
    vl = jnp.logical_and(kl >= wr, kl < wr + NA_WIN_ROWS)
    vr = jnp.logical_and(kl + 1 >= wr, kl + 1 < wr + NA_WIN_ROWS)
    out = NA_ROW_OFFSETS
    return jnp.where(jnp.logical_and(vl, vr), dl,
                     jnp.where(vl, out - 1 + dl, jnp.where(vr, 2 * out - 1 + dl + 1, 3 * out - 1)))


def _na_kernel(q_ref, k_ref, v_ref, bias_ref, o_ref):
    band = pl.program_id(2)
    k0 = pl.multiple_of(_na_key_start(band) * GRID_W, GRID_W)
    nk = NA_KROWS * GRID_W
    codes = [[_na_pair_code(band, i, jp) for jp in range(NA_KROWS // 2)] for i in range(NA_QROWS)]
    for h in range(NA_HEADS_PER_STEP):
        cs = slice(h * NA_HEAD_DIM, (h + 1) * NA_HEAD_DIM)
        k = k_ref[pl.ds(k0, nk), cs]
        v = v_ref[pl.ds(k0, nk), cs]
        bias = jnp.concatenate([jnp.concatenate([bias_ref[h, c] for c in row], axis=1) for row in codes], axis=0)
        s = lax.dot_general(q_ref[:, cs], k, (((1,), (1,)), ((), ())), preferred_element_type=F32)
        s = s * (NA_HEAD_DIM ** -0.5) + bias
        m = jnp.max(s, axis=-1, keepdims=True)
        p = jnp.exp(s - m)
        l = jnp.sum(p, axis=-1, keepdims=True)
        o = jnp.dot(p.astype(BF16), v, preferred_element_type=F32)
        o_ref[:, cs] = (o / l).astype(o_ref.dtype)


def neighbourhood_attention(hm, bias_tab, bsz):
    nbands = ROWS // NA_QROWS
    tq = NA_QROWS * GRID_W
    hp = NA_HEADS_PER_STEP
    wg = hp * NA_HEAD_DIM
    qo, ko, vo = (off // wg for off in (OFF_NA_Q, OFF_NA_K, OFF_NA_V))
    return pl.pallas_call(
        _na_kernel,
        grid=(NA_HEADS // hp, bsz, nbands),
        in_specs=[
            pl.BlockSpec((tq, wg), lambda g, b, r: (b * nbands + r, qo + g)),
            pl.BlockSpec((SEQ, wg), lambda g, b, r: (b, ko + g)),
            pl.BlockSpec((SEQ, wg), lambda g, b, r: (b, vo + g)),
            pl.BlockSpec((hp, NA_PAIR_CODES, GRID_W, 2 * GRID_W), lambda g, b, r: (g, 0, 0, 0)),
        ],
        out_specs=pl.BlockSpec((tq, wg), lambda g, b, r: (b * nbands + r, g)),
        out_shape=jax.ShapeDtypeStruct((bsz * SEQ, BRANCH_WIDTH), BF16),
        compiler_params=_cparams(("parallel", "parallel", "parallel"), 48),
        name="natten",
    )(hm, hm, hm, bias_tab)


def _gla_scan_kernel(qf_ref, kf_ref, vf_ref, af_ref, qb_ref, kb_ref, vb_ref, ab_ref,
                     wdec_ref, bdec_ref, of_ref, ob_ref, state_ref):
    @pl.when(pl.program_id(1) == 0)
    def _():
        state_ref[...] = jnp.zeros_like(state_ref)

    row = lax.broadcasted_iota(jnp.int32, (GLA_CHUNK, GLA_CHUNK), 0)
    col = lax.broadcasted_iota(jnp.int32, (GLA_CHUNK, GLA_CHUNK), 1)
    nt = (((1,), (1,)), ((), ()))
    tn = (((0,), (0,)), ((), ()))
    dirs = ((qf_ref, kf_ref, vf_ref, af_ref, of_ref, row >= col, GLA_CHUNK - 1),
            (qb_ref, kb_ref, vb_ref, ab_ref, ob_ref, row <= col, 0))
    for d, (q_ref, k_ref, v_ref, a_ref, o_ref, tri, last_row) in enumerate(dirs):
        z = jnp.dot(a_ref[...].astype(BF16), wdec_ref[d], preferred_element_type=F32) + bdec_ref[d:d + 1, :]
        log_a = (jnp.minimum(z, 0.0) - jnp.log(1.0 + jnp.exp(-jnp.abs(z)))) * (1.0 / GLA_GATE_TAU)
        cum = jnp.dot(tri.astype(F32), log_a, precision=lax.Precision.HIGHEST, preferred_element_type=F32)
        last = cum[last_row:last_row + 1, :]
        q_in = q_ref[...].astype(F32) * ((GLA_KEY_DIM ** -0.5) * jnp.exp(cum))
        kk = k_ref[...].astype(F32)
        k_in = kk * jnp.exp(-cum)
        k_out = kk * jnp.exp(last - cum)
        dec = jnp.exp(last)
        for h in range(GLA_HEADS):
            ks = slice(h * GLA_KEY_DIM, (h + 1) * GLA_KEY_DIM)
            vs = slice(h * GLA_VALUE_DIM, (h + 1) * GLA_VALUE_DIM)
            qh = q_in[:, ks].astype(BF16)
            vh = v_ref[:, vs]
            att = lax.dot_general(qh, k_in[:, ks].astype(BF16), nt, preferred_element_type=F32)
            att = jnp.where(tri, att, 0.0).astype(BF16)
            st = state_ref[d, h]
            o = jnp.dot(att, vh, preferred_element_type=F32)
            o = o + lax.dot_general(qh, st.astype(BF16), nt, preferred_element_type=F32)
            o_ref[:, vs] = o
            upd = lax.dot_general(vh, k_out[:, ks].astype(BF16), tn, preferred_element_type=F32)
            state_ref[d, h] = st * dec[:, ks] + upd


def gla_scan(hm, a_lr, wdec, bdec, bsz):
    qo = OFF_GLA_Q // GLA_KEY_WIDTH
    ko = OFF_GLA_K // GLA_KEY_WIDTH
    vo = OFF_GLA_V // BRANCH_WIDTH

    def fwd(colblk):
        return lambda b, c: (b * N_CHUNKS + c, colblk)

    def bwd(colblk):
        return lambda b, c: (b * N_CHUNKS + N_CHUNKS - 1 - c, colblk)

    def specs(idx):
        return [pl.BlockSpec((GLA_CHUNK, GLA_KEY_WIDTH), idx(qo)),
                pl.BlockSpec((GLA_CHUNK, GLA_KEY_WIDTH), idx(ko)),
                pl.BlockSpec((GLA_CHUNK, BRANCH_WIDTH), idx(vo)),
                pl.BlockSpec((GLA_CHUNK, LANES), idx(0))]

    out = jax.ShapeDtypeStruct((bsz * SEQ, BRANCH_WIDTH), F32)
    return pl.pallas_call(
        _gla_scan_kernel,
        grid=(bsz, N_CHUNKS),
        in_specs=specs(fwd) + specs(bwd) + [
            pl.BlockSpec((2, LANES, GLA_KEY_WIDTH), lambda b, c: (0, 0, 0)),
            pl.BlockSpec((2, GLA_KEY_WIDTH), lambda b, c: (0, 0))],
        out_specs=[pl.BlockSpec((GLA_CHUNK, BRANCH_WIDTH), fwd(0)),
                   pl.BlockSpec((GLA_CHUNK, BRANCH_WIDTH), bwd(0))],
        out_shape=[out, out],
        scratch_shapes=[pltpu.VMEM((2, GLA_HEADS, GLA_VALUE_DIM, GLA_KEY_DIM), F32)],
        compiler_params=_cparams(("parallel", "arbitrary"), 32),
        name="gla_scan",
    )(hm, hm, hm, a_lr, hm, hm, hm, a_lr, wdec, bdec)


def _gla_out_kernel(of_ref, ob_ref, r_ref, g_ref, o_ref):
    for h in range(GLA_HEADS):
        vs = slice(h * GLA_VALUE_DIM, (h + 1) * GLA_VALUE_DIM)
        o = of_ref[:, vs] + ob_ref[:, vs]
        o = o * lax.rsqrt(jnp.mean(o * o, axis=-1, keepdims=True) + RMS_EPS) * g_ref[:, vs]
        r = r_ref[:, vs].astype(F32)
        o_ref[:, vs] = (o * (r * jax.nn.sigmoid(r))).astype(o_ref.dtype)


def gla_output(o_f, o_b, hm, norm_g, tm=512):
    m = o_f.shape[0]
    row = pl.BlockSpec((tm, BRANCH_WIDTH), lambda i: (i, 0))
    return pl.pallas_call(
        _gla_out_kernel,
        grid=(m // tm,),
        in_specs=[row, row,
                  pl.BlockSpec((tm, BRANCH_WIDTH), lambda i: (i, OFF_GLA_R // BRANCH_WIDTH)),
                  pl.BlockSpec((1, BRANCH_WIDTH), lambda i: (0, 0))],
        out_specs=row,
        out_shape=jax.ShapeDtypeStruct((m, BRANCH_WIDTH), BF16),
        compiler_params=_cparams(("parallel",), 40),
        name="gla_output",
    )(o_f, o_b, hm, norm_g.reshape(1, BRANCH_WIDTH))


def _mix_kernel(y0_ref, y1_ref, y2_ref, w_ref, g0_ref, g1_ref, g2_ref, o_ref):
    acc = None
    for i, (y_ref, g_ref) in enumerate(((y0_ref, g0_ref), (y1_ref, g1_ref), (y2_ref, g2_ref))):
        t = g_ref[...].astype(F32) * jnp.dot(y_ref[...], w_ref[i], preferred_element_type=F32)
        acc = t if acc is None else acc + t
    o_ref[...] = acc.astype(o_ref.dtype)


def gated_mix(ys, w_branch, layer, gates, tm=MM_TILE, tn=512):
    m = ys[0].shape[0]
    nj = D_MODEL // tn
    yspec = pl.BlockSpec((tm, BRANCH_WIDTH), lambda i, j: (i, 0))

    def gspec(br):
        return pl.BlockSpec((tm, tn), lambda i, j: (i, br * nj + j))

    return pl.pallas_call(
        _mix_kernel,
        grid=(m // tm, nj),
        in_specs=[yspec, yspec, yspec,
                  pl.BlockSpec((None, N_BRANCH, BRANCH_WIDTH, tn), lambda i, j: (layer, 0, 0, j)),
                  gspec(0), gspec(1), gspec(2)],
        out_specs=pl.BlockSpec((tm, tn), lambda i, j: (i, j)),
        out_shape=jax.ShapeDtypeStruct((m, D_MODEL), BF16),
        compiler_params=_cparams(("parallel", "parallel"), 52),
        name="gated_mix",
    )(*ys, w_branch, gates, gates, gates)


def _ln_kernel(*refs, alpha):
    *ins, g_ref, b_ref, of_ref, ob_ref = refs
    y = ins[-1][...]
    if len(ins) == 2:
        y = alpha * ins[0][...] + y
    mu = jnp.mean(y, axis=-1, keepdims=True)
    yc = y - mu
    var = jnp.mean(yc * yc, axis=-1, keepdims=True)
    out = yc * lax.rsqrt(var + LN_EPS) * g_ref[...] + b_ref[...]
    of_ref[...] = out
    ob_ref[...] = out.astype(BF16)


def residual_layer_norm(x, s, g, b, alpha, tm=256):
    m, d = s.shape
    row = pl.BlockSpec((tm, d), lambda i: (i, 0))
    vec = pl.BlockSpec((1, d), lambda i: (0, 0))
    acts = [s] if x is None else [x, s]
    return pl.pallas_call(
        functools.partial(_ln_kernel, alpha=alpha),
        grid=(m // tm,),
        in_specs=[row] * len(acts) + [vec, vec],
        out_specs=[row, row],
        out_shape=[jax.ShapeDtypeStruct((m, d), F32), jax.ShapeDtypeStruct((m, d), BF16)],
        compiler_params=_cparams(("parallel",), 40),
        name="residual_ln",
    )(*acts, g.reshape(1, d), b.reshape(1, d))


def kernel(x, w_in, b_gate, w_short_conv, na_rpb, gla_w_decay, gla_b_decay, gla_norm_g, w_branch, w_out,
           ln1_g, ln1_b, w_ffn_in, w_ffn_conv, b_ffn_conv, w_ffn_out, ln2_g, ln2_b):
    bsz, s, d = x.shape
    depth = w_in.shape[0]
    assert (s, d) == (SEQ, D_MODEL) and w_in.shape[2] == OFF_GATE + N_BRANCH * D_MODEL
    assert w_ffn_in.shape[2] == 2 * D_FF
    alpha = (2.0 * depth) ** 0.25
    ff_pad = D_FF_PAD - D_FF

    w_in_t = jnp.swapaxes(w_in, 1, 2)
    w_ff_b, w_ffo_b = prep_w_ffn(w_ffn_in, w_ffn_out)
    w_br_b = w_branch.astype(BF16)
    w_out_b = w_out.astype(BF16)
    w_conv_ff = jnp.pad(w_ffn_conv, ((0, 0), (0, 0), (0, ff_pad)))
    b_conv_ff = jnp.pad(b_ffn_conv, ((0, 0), (0, ff_pad)))

    xf = x.reshape(bsz * s, d)
    xb = xf.astype(BF16)
    for l in range(depth):
        hm = matmul_wres(xb, w_in_t, l, 0, MAIN_WIDTH, BF16)
        gates = matmul_wres(xb, w_in_t, l, OFF_GATE, N_BRANCH * D_MODEL, BF16, bias=b_gate[l])
        a_lr = matmul_wres(xb, w_in_t, l, MAIN_WIDTH, LANES, F32, tn=LANES)

        y_conv = short_conv_mixer(hm, w_short_conv[l], bsz)
        y_na = neighbourhood_attention(hm, _na_pair_tables(na_rpb[l]), bsz)
        wdec = jnp.zeros((2, LANES, GLA_KEY_WIDTH), F32)
        wdec = wdec.at[0, :GLA_GATE_RANK].set(gla_w_decay[l, 0])
        wdec = wdec.at[1, GLA_GATE_RANK:2 * GLA_GATE_RANK].set(gla_w_decay[l, 1])
        o_f, o_b = gla_scan(hm, a_lr, wdec.astype(BF16), gla_b_decay[l], bsz)
        y_gla = gla_output(o_f, o_b, hm, gla_norm_g[l])

        mixed = gated_mix((y_conv, y_na, y_gla), w_br_b, l, gates)
        s1 = matmul_residual(mixed, w_out_b, l, xf, alpha)
        xf, xb = residual_layer_norm(None, s1, ln1_g[l], ln1_b[l], alpha)

        g = ffn_in(xb, w_ff_b, l, w_conv_ff[l], b_conv_ff[l].reshape(1, D_FF_PAD), D_FF)
        s2 = matmul_kacc(g, w_ffo_b, l, D_FF)
        xf, xb = residual_layer_norm(xf, s2, ln2_g[l], ln2_b[l], alpha)
    return xf.reshape(bsz, s, d)
```

```python
import functools

import numpy as np
import jax
import jax.numpy as jnp
from jax import lax
from jax.experimental import pallas as pl
from jax.experimental.pallas import tpu as pltpu

F32 = jnp.float32
BF16 = jnp.bfloat16

D_MODEL = 4096
SEQ = 4096
GRID_W = 64
ROWS = SEQ // GRID_W
BRANCH_WIDTH = D_MODEL // 2
N_BRANCH = 3
NA_HEAD_DIM = 128
NA_HEADS = BRANCH_WIDTH // NA_HEAD_DIM
NA_WIN_ROWS = 8
NA_WIN_COLS = 16
NA_QROWS = 8
NA_KROWS = 16
NA_HEADS_PER_STEP = 4
GLA_HEADS = 4
GLA_KEY_WIDTH = BRANCH_WIDTH // 2
GLA_KEY_DIM = GLA_KEY_WIDTH // GLA_HEADS
GLA_VALUE_DIM = BRANCH_WIDTH // GLA_HEADS
GLA_GATE_RANK = 16
GLA_GATE_TAU = 16.0
GLA_CHUNK = 64
N_CHUNKS = SEQ // GLA_CHUNK
D_FF = 256 * ((8 * D_MODEL // 3 + 255) // 256)
LN_EPS = 1e-5
RMS_EPS = 1e-6
NEG_INF = -1e30

OFF_CONV_B = 0
OFF_CONV_C = BRANCH_WIDTH
OFF_CONV_H = 2 * BRANCH_WIDTH
OFF_NA_Q = 3 * BRANCH_WIDTH
OFF_NA_K = 4 * BRANCH_WIDTH
OFF_NA_V = 5 * BRANCH_WIDTH
OFF_GLA_Q = 6 * BRANCH_WIDTH
OFF_GLA_K = OFF_GLA_Q + GLA_KEY_WIDTH
OFF_GLA_V = OFF_GLA_K + GLA_KEY_WIDTH
OFF_GLA_R = OFF_GLA_V + BRANCH_WIDTH
MAIN_WIDTH = OFF_GLA_R + BRANCH_WIDTH
OFF_GATE = MAIN_WIDTH + 2 * GLA_GATE_RANK

LANES = 128
V7X_VMEM_BYTES = 64 * 1024 * 1024
MM_TILE = 1024
D_FF_PAD = MM_TILE * (-(-D_FF // MM_TILE))
FFN_HALO = 16
MIB = 1024 * 1024


def _cparams(semantics, vmem_mib):
    assert vmem_mib * MIB < V7X_VMEM_BYTES
    return pltpu.CompilerParams(dimension_semantics=semantics, vmem_limit_bytes=vmem_mib * MIB)


def _cast_or_zero_kernel(src_ref, o_ref, *, axis, pad_index):
    is_pad = pl.program_id(axis) % (pad_index + 1) == pad_index

    @pl.when(is_pad)
    def _():
        o_ref[...] = jnp.zeros_like(o_ref)

    @pl.when(jnp.logical_not(is_pad))
    def _():
        o_ref[...] = src_ref[...].astype(o_ref.dtype)


def prep_w_ffn(w_ffn_in, w_ffn_out, tb=256):
    depth, d, _ = w_ffn_in.shape
    live, padded = D_FF // tb, D_FF_PAD // tb
    assert live * tb == D_FF and padded == live + 1
    w_in_b = pl.pallas_call(
        functools.partial(_cast_or_zero_kernel, axis=1, pad_index=live),
        grid=(depth, 2 * padded),
        in_specs=[pl.BlockSpec((None, d, tb),
                               lambda l, j: (l, 0, (j // padded) * live + jnp.minimum(j % padded, live - 1)))],
        out_specs=pl.BlockSpec((None, d, tb), lambda l, j: (l, 0, j)),
        out_shape=jax.ShapeDtypeStruct((depth, d, 2 * D_FF_PAD), BF16),
        compiler_params=_cparams(("parallel", "parallel"), 32), name="prep_w_ffn_in")(w_ffn_in)
    w_out_b = pl.pallas_call(
        functools.partial(_cast_or_zero_kernel, axis=1, pad_index=live),
        grid=(depth, padded),
        in_specs=[pl.BlockSpec((None, tb, d), lambda l, i: (l, jnp.minimum(i, live - 1), 0))],
        out_specs=pl.BlockSpec((None, tb, d), lambda l, i: (l, i, 0)),
        out_shape=jax.ShapeDtypeStruct((depth, D_FF_PAD, d), BF16),
        compiler_params=_cparams(("parallel", "parallel"), 32), name="prep_w_ffn_out")(w_ffn_out)
    return w_in_b, w_out_b


_NT = (((1,), (1,)), ((), ()))


def _mm_residual_kernel(a_ref, b_ref, res_ref, o_ref, *, alpha):
    acc = jnp.dot(a_ref[...], b_ref[...], preferred_element_type=F32)
    o_ref[...] = alpha * res_ref[...] + acc


def matmul_residual(a, w, layer, residual, alpha, tm=MM_TILE, tn=512):
    m, k = a.shape
    n = w.shape[2]
    assert n % tn == 0 and m % tm == 0
    return pl.pallas_call(
        functools.partial(_mm_residual_kernel, alpha=alpha),
        grid=(m // tm, n // tn),
        in_specs=[pl.BlockSpec((tm, k), lambda i, j: (i, 0)),
                  pl.BlockSpec((None, k, tn), lambda i, j: (layer, 0, j)),
                  pl.BlockSpec((tm, tn), lambda i, j: (i, j))],
        out_specs=pl.BlockSpec((tm, tn), lambda i, j: (i, j)),
        out_shape=jax.ShapeDtypeStruct((m, n), F32),
        compiler_params=_cparams(("parallel", "parallel"), 40),
        name="mm_residual",
    )(a, w, residual)


def _mm_wres_kernel(a_ref, w_ref, o_ref):
    acc = lax.dot_general(a_ref[...], w_ref[...].astype(BF16), _NT, preferred_element_type=F32)
    o_ref[...] = acc.astype(o_ref.dtype)


def _mm_wres_sigmoid_kernel(a_ref, w_ref, bias_ref, o_ref):
    acc = lax.dot_general(a_ref[...], w_ref[...].astype(BF16), _NT, preferred_element_type=F32)
    o_ref[...] = jax.nn.sigmoid(acc + bias_ref[...]).astype(o_ref.dtype)


def matmul_wres(a, w_t, layer, row0, n, out_dtype, bias=None, tm=MM_TILE, tn=768):
    m, k = a.shape
    assert n % tn == 0 and m % tm == 0 and row0 % 8 == 0
    in_specs = [pl.BlockSpec((tm, k), lambda j, i: (i, 0)),
                pl.BlockSpec((None, pl.Element(tn), pl.Element(k)),
                             lambda j, i: (layer, pl.multiple_of(row0 + j * tn, 8), 0))]
    args = [a, w_t]
    body = _mm_wres_kernel
    if bias is not None:
        in_specs.append(pl.BlockSpec((1, tn), lambda j, i: (0, j)))
        args.append(bias.reshape(1, n))
        body = _mm_wres_sigmoid_kernel
    return pl.pallas_call(
        body,
        grid=(n // tn, m // tm),
        in_specs=in_specs,
        out_specs=pl.BlockSpec((tm, tn), lambda j, i: (i, j)),
        out_shape=jax.ShapeDtypeStruct((m, n), out_dtype),
        compiler_params=_cparams(("parallel", "parallel"), 52),
        name="mm_wres" if bias is None else "mm_wres_sigmoid",
    )(*args)


def _mm_kacc_kernel(a_ref, b_ref, o_ref, *, k_last):
    kk = pl.program_id(1)
    last = pl.num_programs(1) - 1

    @pl.when(kk == 0)
    def _():
        o_ref[...] = jnp.zeros_like(o_ref)

    @pl.when(kk < last)
    def _():
        o_ref[...] += jnp.dot(a_ref[...], b_ref[...], preferred_element_type=F32)

    @pl.when(kk == last)
    def _():
        o_ref[...] += jnp.dot(a_ref[:, :k_last], b_ref[:k_last, :], preferred_element_type=F32)


def matmul_kacc(a, w, layer, k_live, tm=MM_TILE, tk=MM_TILE):
    m, k = a.shape
    n = w.shape[2]
    assert k % tk == 0 and 0 < k - k_live < tk
    return pl.pallas_call(
        functools.partial(_mm_kacc_kernel, k_last=k_live - (k - tk)),
        grid=(m // tm, k // tk),
        in_specs=[pl.BlockSpec((tm, tk), lambda i, kk: (i, kk)),
                  pl.BlockSpec((None, tk, n), lambda i, kk: (layer, kk, 0))],
        out_specs=pl.BlockSpec((tm, n), lambda i, kk: (i, 0)),
        out_shape=jax.ShapeDtypeStruct((m, n), F32),
        compiler_params=_cparams(("parallel", "arbitrary"), 56),
        name="mm_kacc",
    )(a, w)


def _conv3(p, w_ref):
    t = lax.broadcasted_iota(jnp.int32, p.shape, 0)
    prev = jnp.where(t == 0, 0.0, pltpu.roll(p, 1, 0))
    nxt = jnp.where(t == SEQ - 1, 0.0, pltpu.roll(p, SEQ - 1, 0))
    return prev * w_ref[0:1, :] + p * w_ref[1:2, :] + nxt * w_ref[2:3, :]


def _short_conv_kernel(b_ref, c_ref, h_ref, w_ref, o_ref):
    p = c_ref[...].astype(F32) * h_ref[...].astype(F32)
    o_ref[...] = (b_ref[...].astype(F32) * _conv3(p, w_ref)).astype(o_ref.dtype)


def short_conv_mixer(hm, w_conv, bsz, tc=256):
    nb = BRANCH_WIDTH // tc

    def spec(off):
        return pl.BlockSpec((SEQ, tc), lambda b, j: (b, off // tc + j))

    return pl.pallas_call(
        _short_conv_kernel,
        grid=(bsz, nb),
        in_specs=[spec(OFF_CONV_B), spec(OFF_CONV_C), spec(OFF_CONV_H),
                  pl.BlockSpec((3, tc), lambda b, j: (0, j))],
        out_specs=pl.BlockSpec((SEQ, tc), lambda b, j: (b, j)),
        out_shape=jax.ShapeDtypeStruct((bsz * SEQ, BRANCH_WIDTH), BF16),
        compiler_params=_cparams(("parallel", "parallel"), 48),
        name="short_conv",
    )(hm, hm, hm, w_conv)


def _ffn_in_kernel(xp_ref, x_ref, xn_ref, wa_ref, wu_ref, wc_ref, bc_ref, o_ref, xext_ref, *,
                   tiles_per_seq, n_last):
    i = pl.program_id(0)
    tm = x_ref.shape[0]
    n_ext = tm + 2 * FFN_HALO

    @pl.when(pl.program_id(1) == 0)
    def _():
        xext_ref[0:FFN_HALO] = xp_ref[...]
        xext_ref[FFN_HALO:FFN_HALO + tm] = x_ref[...]
        xext_ref[FFN_HALO + tm:n_ext] = xn_ref[...]

    def compute(ncols):
        cs = slice(0, ncols)
        a_ext = jnp.dot(xext_ref[...], wa_ref[:, cs], preferred_element_type=F32)
        u = jnp.dot(x_ref[...], wu_ref[:, cs], preferred_element_type=F32)
        a = a_ext[FFN_HALO:FFN_HALO + tm]
        prev = pltpu.roll(a_ext, 1, 0)[FFN_HALO:FFN_HALO + tm]
        nxt = pltpu.roll(a_ext, n_ext - 1, 0)[FFN_HALO:FFN_HALO + tm]
        pos = lax.rem(i, tiles_per_seq)
        row = lax.broadcasted_iota(jnp.int32, a.shape, 0)
        prev = jnp.where(row == jnp.where(pos == 0, 0, -1), 0.0, prev)
        nxt = jnp.where(row == jnp.where(pos == tiles_per_seq - 1, tm - 1, -1), 0.0, nxt)
        c = prev * wc_ref[0:1, cs] + a * wc_ref[1:2, cs] + nxt * wc_ref[2:3, cs] + bc_ref[:, cs]
        g = 0.5 * c * (1.0 + lax.erf(c * (2.0 ** -0.5)))
        o_ref[:, cs] = (g * u).astype(o_ref.dtype)
        if ncols < o_ref.shape[1]:
            o_ref[:, ncols:] = jnp.zeros((tm, o_ref.shape[1] - ncols), o_ref.dtype)

    j = pl.program_id(1)
    last = pl.num_programs(1) - 1
    pl.when(j < last)(lambda: compute(o_ref.shape[1]))
    pl.when(j == last)(lambda: compute(n_last))


def ffn_in(xb, w_ff, layer, w_conv, b_conv, n_live, tm=MM_TILE, tn=512):
    m, k = xb.shape
    nf = w_ff.shape[2] // 2
    nb = nf // tn
    hb = tm // FFN_HALO
    assert nf % tn == 0 and 0 < nf - n_live < tn
    return pl.pallas_call(
        functools.partial(_ffn_in_kernel, tiles_per_seq=SEQ // tm, n_last=n_live - (nf - tn)),
        grid=(m // tm, nb),
        in_specs=[pl.BlockSpec((FFN_HALO, k), lambda i, j: (jnp.maximum(i * hb - 1, 0), 0)),
                  pl.BlockSpec((tm, k), lambda i, j: (i, 0)),
                  pl.BlockSpec((FFN_HALO, k), lambda i, j: (jnp.minimum((i + 1) * hb, m // FFN_HALO - 1), 0)),
                  pl.BlockSpec((None, k, tn), lambda i, j: (layer, 0, j)),
                  pl.BlockSpec((None, k, tn), lambda i, j: (layer, 0, nb + j)),
                  pl.BlockSpec((3, tn), lambda i, j: (0, j)),
                  pl.BlockSpec((1, tn), lambda i, j: (0, j))],
        out_specs=pl.BlockSpec((tm, tn), lambda i, j: (i, j)),
        out_shape=jax.ShapeDtypeStruct((m, nf), BF16),
        scratch_shapes=[pltpu.VMEM((tm + 2 * FFN_HALO, k), BF16)],
        compiler_params=_cparams(("parallel", "arbitrary"), 52),
        name="ffn_in",
    )(xb, xb, xb, w_ff, w_ff, w_conv, b_conv)


def _na_key_start(band):
    lo = band * NA_QROWS - NA_WIN_ROWS // 2
    hi = ROWS - NA_KROWS
    if isinstance(band, int):
        return min(max(lo, 0), hi)
    return jnp.clip(lo, 0, hi)


NA_SCALE = NA_HEAD_DIM ** -0.5
LOG2_E = 1.4426950408889634
NA_ROW_OFFSETS = 2 * NA_WIN_ROWS - 1
NA_PAIR_CODES = 3 * NA_ROW_OFFSETS


def _na_pair_tables(rpb):
    qc = np.arange(GRID_W)[:, None]
    kc = np.arange(GRID_W)[None, :]
    wc = np.clip(qc - NA_WIN_COLS // 2, 0, GRID_W - NA_WIN_COLS)
    col_valid = (kc >= wc) & (kc < wc + NA_WIN_COLS)
    cidx = np.clip(kc - qc + NA_WIN_COLS - 1, 0, 2 * NA_WIN_COLS - 2)
    tm = jnp.where(col_valid[None, None], rpb[:, :, cidx] * (1.0 / NA_SCALE), NEG_INF)
    tm = jnp.concatenate([tm, jnp.full_like(tm[:, :1], NEG_INF)], axis=1)
    out = NA_ROW_OFFSETS
    both = np.arange(out - 1)
    one = np.arange(out)
    left = np.concatenate([both, one, np.full(out, out), [out]])
    right = np.concatenate([both + 1, np.full(out, out), one, [out]])
    return jnp.concatenate([tm[:, left], tm[:, right]], axis=-1)


def _na_pair_code(band, i, jp):
    r = band * NA_QROWS + i
    wr = jnp.clip(r - NA_WIN_ROWS // 2, 0, ROWS - NA_WIN_ROWS)
    kl = _na_key_start(band) + 2 * jp
    dl = kl - r + NA_WIN_ROWS - 1
    vl = jnp.logical_and(kl >= wr, kl < wr + NA_WIN_ROWS)
    vr = jnp.logical_and(kl + 1 >= wr, kl + 1 < wr + NA_WIN_ROWS)
    out = NA_ROW_OFFSETS
    return jnp.where(jnp.logical_and(vl, vr), dl,
                     jnp.where(vl, out - 1 + dl, jnp.where(vr, 2 * out - 1 + dl + 1, 3 * out - 1)))


def _na_kernel(q_ref, k_ref, v_ref, bias_ref, o_ref):
    band = pl.program_id(2)
    k0 = pl.multiple_of(_na_key_start(band) * GRID_W, GRID_W)
    nk = NA_KROWS * GRID_W
    codes = [[_na_pair_code(band, i, jp) for jp in range(NA_KROWS // 2)] for i in range(NA_QROWS)]
    for h in range(NA_HEADS_PER_STEP):
        cs = slice(h * NA_HEAD_DIM, (h + 1) * NA_HEAD_DIM)
        k = k_ref[pl.ds(k0, nk), cs]
        v = v_ref[pl.ds(k0, nk), cs]
        bias = jnp.concatenate([jnp.concatenate([bias_ref[h, c] for c in row], axis=1) for row in codes], axis=0)
        t = lax.dot_general(q_ref[:, cs], k, (((1,), (1,)), ((), ())), preferred_element_type=F32) + bias
        m = jnp.max(t, axis=-1, keepdims=True)
        p = jnp.exp2((t - m) * (NA_SCALE * LOG2_E))
        l = jnp.sum(p, axis=-1, keepdims=True)
        o = jnp.dot(p.astype(BF16), v, preferred_element_type=F32)
        o_ref[:, cs] = (o / l).astype(o_ref.dtype)


def neighbourhood_attention(hm, bias_tab, bsz):
    nbands = ROWS // NA_QROWS
    tq = NA_QROWS * GRID_W
    hp = NA_HEADS_PER_STEP
    wg = hp * NA_HEAD_DIM
    qo, ko, vo = (off // wg for off in (OFF_NA_Q, OFF_NA_K, OFF_NA_V))
    return pl.pallas_call(
        _na_kernel,
        grid=(NA_HEADS // hp, bsz, nbands),
        in_specs=[
            pl.BlockSpec((tq, wg), lambda g, b, r: (b * nbands + r, qo + g)),
            pl.BlockSpec((SEQ, wg), lambda g, b, r: (b, ko + g)),
            pl.BlockSpec((SEQ, wg), lambda g, b, r: (b, vo + g)),
            pl.BlockSpec((hp, NA_PAIR_CODES, GRID_W, 2 * GRID_W), lambda g, b, r: (g, 0, 0, 0)),
        ],
        out_specs=pl.BlockSpec((tq, wg), lambda g, b, r: (b * nbands + r, g)),
        out_shape=jax.ShapeDtypeStruct((bsz * SEQ, BRANCH_WIDTH), BF16),
        compiler_params=_cparams(("parallel", "parallel", "parallel"), 48),
        name="natten",
    )(hm, hm, hm, bias_tab)


def _gla_scan_kernel(qf_ref, kf_ref, vf_ref, af_ref, qb_ref, kb_ref, vb_ref, ab_ref,
                     wdec_ref, bdec_ref, of_ref, ob_ref, state_ref):
    @pl.when(pl.program_id(1) == 0)
    def _():
        state_ref[...] = jnp.zeros_like(state_ref)

    row = lax.broadcasted_iota(jnp.int32, (GLA_CHUNK, GLA_CHUNK), 0)
    col = lax.broadcasted_iota(jnp.int32, (GLA_CHUNK, GLA_CHUNK), 1)
    nt = (((1,), (1,)), ((), ()))
    tn = (((0,), (0,)), ((), ()))
    dirs = ((qf_ref, kf_ref, vf_ref, af_ref, of_ref, row >= col, GLA_CHUNK - 1),
            (qb_ref, kb_ref, vb_ref, ab_ref, ob_ref, row <= col, 0))
    for d, (q_ref, k_ref, v_ref, a_ref, o_ref, tri, last_row) in enumerate(dirs):
        z = jnp.dot(a_ref[...].astype(BF16), wdec_ref[d], preferred_element_type=F32) + bdec_ref[d:d + 1, :]
        log_a = (jnp.minimum(z, 0.0) - jnp.log(1.0 + jnp.exp(-jnp.abs(z)))) * (1.0 / GLA_GATE_TAU)
        cum = jnp.dot(tri.astype(F32), log_a, precision=lax.Precision.HIGHEST, preferred_element_type=F32)
        last = cum[last_row:last_row + 1, :]
        q_in = q_ref[...].astype(F32) * ((GLA_KEY_DIM ** -0.5) * jnp.exp(cum))
        kk = k_ref[...].astype(F32)
        k_in = kk * jnp.exp(-cum)
        k_out = kk * jnp.exp(last - cum)
        dec = jnp.exp(last)
        for h in range(GLA_HEADS):
            ks = slice(h * GLA_KEY_DIM, (h + 1) * GLA_KEY_DIM)
            vs = slice(h * GLA_VALUE_DIM, (h + 1) * GLA_VALUE_DIM)
            qh = q_in[:, ks].astype(BF16)
            vh = v_ref[:, vs]
            att = lax.dot_general(qh, k_in[:, ks].astype(BF16), nt, preferred_element_type=F32)
            att = jnp.where(tri, att, 0.0).astype(BF16)
            st = state_ref[d, h]
            o = jnp.dot(att, vh, preferred_element_type=F32)
            o = o + lax.dot_general(qh, st.astype(BF16), nt, preferred_element_type=F32)
            o_ref[:, vs] = o
            upd = lax.dot_general(vh, k_out[:, ks].astype(BF16), tn, preferred_element_type=F32)
            state_ref[d, h] = st * dec[:, ks] + upd


def gla_scan(hm, a_lr, wdec, bdec, bsz):
    qo = OFF_GLA_Q // GLA_KEY_WIDTH
    ko = OFF_GLA_K // GLA_KEY_WIDTH
    vo = OFF_GLA_V // BRANCH_WIDTH

    def fwd(colblk):
        return lambda b, c: (b * N_CHUNKS + c, colblk)

    def bwd(colblk):
        return lambda b, c: (b * N_CHUNKS + N_CHUNKS - 1 - c, colblk)

    def specs(idx):
        return [pl.BlockSpec((GLA_CHUNK, GLA_KEY_WIDTH), idx(qo)),
                pl.BlockSpec((GLA_CHUNK, GLA_KEY_WIDTH), idx(ko)),
                pl.BlockSpec((GLA_CHUNK, BRANCH_WIDTH), idx(vo)),
                pl.BlockSpec((GLA_CHUNK, LANES), idx(0))]

    out = jax.ShapeDtypeStruct((bsz * SEQ, BRANCH_WIDTH), F32)
    return pl.pallas_call(
        _gla_scan_kernel,
        grid=(bsz, N_CHUNKS),
        in_specs=specs(fwd) + specs(bwd) + [
            pl.BlockSpec((2, LANES, GLA_KEY_WIDTH), lambda b, c: (0, 0, 0)),
            pl.BlockSpec((2, GLA_KEY_WIDTH), lambda b, c: (0, 0))],
        out_specs=[pl.BlockSpec((GLA_CHUNK, BRANCH_WIDTH), fwd(0)),
                   pl.BlockSpec((GLA_CHUNK, BRANCH_WIDTH), bwd(0))],
        out_shape=[out, out],
        scratch_shapes=[pltpu.VMEM((2, GLA_HEADS, GLA_VALUE_DIM, GLA_KEY_DIM), F32)],
        compiler_params=_cparams(("parallel", "arbitrary"), 32),
        name="gla_scan",
    )(hm, hm, hm, a_lr, hm, hm, hm, a_lr, wdec, bdec)


def _gla_out_kernel(of_ref, ob_ref, r_ref, g_ref, o_ref):
    for h in range(GLA_HEADS):
        vs = slice(h * GLA_VALUE_DIM, (h + 1) * GLA_VALUE_DIM)
        o = of_ref[:, vs] + ob_ref[:, vs]
        o = o * lax.rsqrt(jnp.mean(o * o, axis=-1, keepdims=True) + RMS_EPS) * g_ref[:, vs]
        r = r_ref[:, vs].astype(F32)
        o_ref[:, vs] = (o * (r * jax.nn.sigmoid(r))).astype(o_ref.dtype)


def gla_output(o_f, o_b, hm, norm_g, tm=512):
    m = o_f.shape[0]
    row = pl.BlockSpec((tm, BRANCH_WIDTH), lambda i: (i, 0))
    return pl.pallas_call(
        _gla_out_kernel,
        grid=(m // tm,),
        in_specs=[row, row,
                  pl.BlockSpec((tm, BRANCH_WIDTH), lambda i: (i, OFF_GLA_R // BRANCH_WIDTH)),
                  pl.BlockSpec((1, BRANCH_WIDTH), lambda i: (0, 0))],
        out_specs=row,
        out_shape=jax.ShapeDtypeStruct((m, BRANCH_WIDTH), BF16),
        compiler_params=_cparams(("parallel",), 40),
        name="gla_output",
    )(o_f, o_b, hm, norm_g.reshape(1, BRANCH_WIDTH))


def _mix_kernel(y0_ref, y1_ref, y2_ref, w_ref, g0_ref, g1_ref, g2_ref, o_ref):
    acc = None
    for i, (y_ref, g_ref) in enumerate(((y0_ref, g0_ref), (y1_ref, g1_ref), (y2_ref, g2_ref))):
        t = g_ref[...].astype(F32) * jnp.dot(y_ref[...], w_ref[i], preferred_element_type=F32)
        acc = t if acc is None else acc + t
    o_ref[...] = acc.astype(o_ref.dtype)


def gated_mix(ys, w_branch, layer, gates, tm=MM_TILE, tn=512):
    m = ys[0].shape[0]
    nj = D_MODEL // tn
    yspec = pl.BlockSpec((tm, BRANCH_WIDTH), lambda i, j: (i, 0))

    def gspec(br):
        return pl.BlockSpec((tm, tn), lambda i, j: (i, br * nj + j))

    return pl.pallas_call(
        _mix_kernel,
        grid=(m // tm, nj),
        in_specs=[yspec, yspec, yspec,
                  pl.BlockSpec((None, N_BRANCH, BRANCH_WIDTH, tn), lambda i, j: (layer, 0, 0, j)),
                  gspec(0), gspec(1), gspec(2)],
        out_specs=pl.BlockSpec((tm, tn), lambda i, j: (i, j)),
        out_shape=jax.ShapeDtypeStruct((m, D_MODEL), BF16),
        compiler_params=_cparams(("parallel", "parallel"), 52),
        name="gated_mix",
    )(*ys, w_branch, gates, gates, gates)


def _layer_norm(y, g_ref, b_ref):
    mu = jnp.mean(y, axis=-1, keepdims=True)
    yc = y - mu
    var = jnp.mean(yc * yc, axis=-1, keepdims=True)
    return yc * lax.rsqrt(var + LN_EPS) * g_ref[...] + b_ref[...]


def _ln1_kernel(s_ref, g_ref, b_ref, ob_ref):
    ob_ref[...] = _layer_norm(s_ref[...], g_ref, b_ref).astype(BF16)


def _ln2_kernel(s1_ref, s2_ref, g1_ref, b1_ref, g2_ref, b2_ref, of_ref, ob_ref, *, alpha):
    x1 = _layer_norm(s1_ref[...], g1_ref, b1_ref)
    out = _layer_norm(alpha * x1 + s2_ref[...], g2_ref, b2_ref)
    of_ref[...] = out
    ob_ref[...] = out.astype(BF16)


LN_ROWS = 256


def layer_norm_bf16(s1, g1, b1):
    m, d = s1.shape
    row = pl.BlockSpec((LN_ROWS, d), lambda i: (i, 0))
    vec = pl.BlockSpec((1, d), lambda i: (0, 0))
    return pl.pallas_call(
        _ln1_kernel,
        grid=(m // LN_ROWS,),
        in_specs=[row, vec, vec],
        out_specs=row,
        out_shape=jax.ShapeDtypeStruct((m, d), BF16),
        compiler_params=_cparams(("parallel",), 40),
        name="ln1",
    )(s1, g1.reshape(1, d), b1.reshape(1, d))


def layer_norm_ffn_residual(s1, s2, g1, b1, g2, b2, alpha):
    m, d = s1.shape
    row = pl.BlockSpec((LN_ROWS, d), lambda i: (i, 0))
    vec = pl.BlockSpec((1, d), lambda i: (0, 0))
    return pl.pallas_call(
        functools.partial(_ln2_kernel, alpha=alpha),
        grid=(m // LN_ROWS,),
        in_specs=[row, row, vec, vec, vec, vec],
        out_specs=[row, row],
        out_shape=[jax.ShapeDtypeStruct((m, d), F32), jax.ShapeDtypeStruct((m, d), BF16)],
        compiler_params=_cparams(("parallel",), 40),
        name="ln2",
    )(s1, s2, g1.reshape(1, d), b1.reshape(1, d), g2.reshape(1, d), b2.reshape(1, d))


def kernel(x, w_in, b_gate, w_short_conv, na_rpb, gla_w_decay, gla_b_decay, gla_norm_g, w_branch, w_out,
           ln1_g, ln1_b, w_ffn_in, w_ffn_conv, b_ffn_conv, w_ffn_out, ln2_g, ln2_b):
    bsz, s, d = x.shape
    depth = w_in.shape[0]
    assert (s, d) == (SEQ, D_MODEL) and w_in.shape[2] == OFF_GATE + N_BRANCH * D_MODEL
    assert w_ffn_in.shape[2] == 2 * D_FF
    alpha = (2.0 * depth) ** 0.25
    ff_pad = D_FF_PAD - D_FF

    w_in_t = jnp.swapaxes(w_in, 1, 2)
    w_ff_b, w_ffo_b = prep_w_ffn(w_ffn_in, w_ffn_out)
    w_br_b = w_branch.astype(BF16)
    w_out_b = w_out.astype(BF16)
    w_conv_ff = jnp.pad(w_ffn_conv, ((0, 0), (0, 0), (0, ff_pad)))
    b_conv_ff = jnp.pad(b_ffn_conv, ((0, 0), (0, ff_pad)))

    xf = x.reshape(bsz * s, d)
    xb = xf.astype(BF16)
    for l in range(depth):
        hm = matmul_wres(xb, w_in_t, l, 0, MAIN_WIDTH, BF16)
        gates = matmul_wres(xb, w_in_t, l, OFF_GATE, N_BRANCH * D_MODEL, BF16, bias=b_gate[l])
        a_lr = matmul_wres(xb, w_in_t, l, MAIN_WIDTH, LANES, F32, tn=LANES)

        y_conv = short_conv_mixer(hm, w_short_conv[l], bsz)
        y_na = neighbourhood_attention(hm, _na_pair_tables(na_rpb[l]), bsz)
        wdec = jnp.zeros((2, LANES, GLA_KEY_WIDTH), F32)
        wdec = wdec.at[0, :GLA_GATE_RANK].set(gla_w_decay[l, 0])
        wdec = wdec.at[1, GLA_GATE_RANK:2 * GLA_GATE_RANK].set(gla_w_decay[l, 1])
        o_f, o_b = gla_scan(hm, a_lr, wdec.astype(BF16), gla_b_decay[l], bsz)
        y_gla = gla_output(o_f, o_b, hm, gla_norm_g[l])

        mixed = gated_mix((y_conv, y_na, y_gla), w_br_b, l, gates)
        s1 = matmul_residual(mixed, w_out_b, l, xf, alpha)
        xb = layer_norm_bf16(s1, ln1_g[l], ln1_b[l])

        g = ffn_in(xb, w_ff_b, l, w_conv_ff[l], b_conv_ff[l].reshape(1, D_FF_PAD), D_FF)
        s2 = matmul_kacc(g, w_ffo_b, l, D_FF)
        xf, xb = layer_norm_ffn_residual(s1, s2, ln1_g[l], ln1_b[l], ln2_g[l], ln2_b[l], alpha)
    return xf.reshape(bsz, s, d)
```

```python
import functools
from typing import Callable, NamedTuple

import numpy as np
import jax
import jax.numpy as jnp
from jax import lax
from jax.experimental import pallas as pl
from jax.experimental.pallas import tpu as pltpu

F32 = jnp.float32
BF16 = jnp.bfloat16

D_MODEL = 4096
SEQ = 4096
GRID_W = 64
ROWS = SEQ // GRID_W
BRANCH_WIDTH = D_MODEL // 2
N_BRANCH = 3
NA_HEAD_DIM = 128
NA_HEADS = BRANCH_WIDTH // NA_HEAD_DIM
NA_WIN_ROWS = 8
NA_WIN_COLS = 16
NA_QROWS = 8
NA_KROWS = 16
NA_HEADS_PER_STEP = 4
GLA_HEADS = 4
GLA_KEY_WIDTH = BRANCH_WIDTH // 2
GLA_KEY_DIM = GLA_KEY_WIDTH // GLA_HEADS
GLA_VALUE_DIM = BRANCH_WIDTH // GLA_HEADS
GLA_GATE_RANK = 16
GLA_GATE_TAU = 16.0
GLA_CHUNK = 64
N_CHUNKS = SEQ // GLA_CHUNK
D_FF = 256 * ((8 * D_MODEL // 3 + 255) // 256)
LN_EPS = 1e-5
RMS_EPS = 1e-6
NEG_INF = -1e30

OFF_CONV_B = 0
OFF_CONV_C = BRANCH_WIDTH
OFF_CONV_H = 2 * BRANCH_WIDTH
OFF_NA_Q = 3 * BRANCH_WIDTH
OFF_NA_K = 4 * BRANCH_WIDTH
OFF_NA_V = 5 * BRANCH_WIDTH
OFF_GLA_Q = 6 * BRANCH_WIDTH
OFF_GLA_K = OFF_GLA_Q + GLA_KEY_WIDTH
OFF_GLA_V = OFF_GLA_K + GLA_KEY_WIDTH
OFF_GLA_R = OFF_GLA_V + BRANCH_WIDTH
MAIN_WIDTH = OFF_GLA_R + BRANCH_WIDTH
OFF_GATE = MAIN_WIDTH + 2 * GLA_GATE_RANK

LANES = 128
V7X_VMEM_BYTES = 64 * 1024 * 1024
MM_TILE = 1024
D_FF_PAD = MM_TILE * (-(-D_FF // MM_TILE))
FFN_HALO = 16
MIB = 1024 * 1024


def _cparams(semantics, vmem_mib):
    assert vmem_mib * MIB < V7X_VMEM_BYTES
    return pltpu.CompilerParams(dimension_semantics=semantics, vmem_limit_bytes=vmem_mib * MIB)


_NT = (((1,), (1,)), ((), ()))


def _mm_residual_kernel(a_ref, b_ref, res_ref, o_ref, *, alpha):
    acc = jnp.dot(a_ref[...], b_ref[...], preferred_element_type=F32)
    o_ref[...] = alpha * res_ref[...] + acc


def matmul_residual(a, w, layer, residual, alpha, tm=MM_TILE, tn=512):
    m, k = a.shape
    n = w.shape[2]
    assert n % tn == 0 and m % tm == 0
    return pl.pallas_call(
        functools.partial(_mm_residual_kernel, alpha=alpha),
        grid=(m // tm, n // tn),
        in_specs=[pl.BlockSpec((tm, k), lambda i, j: (i, 0)),
                  pl.BlockSpec((None, k, tn), lambda i, j: (layer, 0, j)),
                  pl.BlockSpec((tm, tn), lambda i, j: (i, j))],
        out_specs=pl.BlockSpec((tm, tn), lambda i, j: (i, j)),
        out_shape=jax.ShapeDtypeStruct((m, n), F32),
        compiler_params=_cparams(("parallel", "parallel"), 40),
        name="mm_residual",
    )(a, w, residual)


class PadCast(NamedTuple):
    src: jax.Array
    out_shape: tuple
    block: tuple
    n_blocks: int
    plan: Callable


def ffn_in_padcast(w_ffn_in, cb=LANES):
    depth, d, _ = w_ffn_in.shape
    live, padded = D_FF // cb, D_FF_PAD // cb

    def plan(t):
        layer, c = t // (2 * padded), t % (2 * padded)
        half, jj = c // padded, c % padded
        return (layer, 0, half * live + jnp.minimum(jj, live - 1)), (layer, 0, c), jj >= live

    return PadCast(w_ffn_in, (depth, d, 2 * D_FF_PAD), (None, d, cb), depth * 2 * padded, plan)


def ffn_out_padcast(w_ffn_out, rb=256):
    depth, _, d = w_ffn_out.shape
    live, padded = D_FF // rb, D_FF_PAD // rb

    def plan(t):
        layer, r = t // padded, t % padded
        return (layer, jnp.minimum(r, live - 1), 0), (layer, r, 0), r >= live

    return PadCast(w_ffn_out, (depth, D_FF_PAD, d), (None, rb, d), depth * padded, plan)


def _mm_wres_kernel(*refs, sigmoid, cast_plan, cast_blocks):
    a_ref, w_ref = refs[:2]
    refs = refs[2:]
    acc = lax.dot_general(a_ref[...], w_ref[...].astype(BF16), _NT, preferred_element_type=F32)
    if sigmoid:
        acc = jax.nn.sigmoid(acc + refs[0][...])
        refs = refs[1:]
    if cast_plan is None:
        (o_ref,) = refs
    else:
        src_ref, o_ref, dst_ref = refs
        t = jnp.minimum(pl.program_id(0) * pl.num_programs(1) + pl.program_id(1), cast_blocks - 1)
        is_pad = cast_plan(t)[2]
        rows = lax.broadcasted_iota(jnp.int32, src_ref.shape, 0)
        live = rows < jnp.where(is_pad, 0, src_ref.shape[0])
        dst_ref[...] = jnp.where(live, src_ref[...], 0.0).astype(dst_ref.dtype)
    o_ref[...] = acc.astype(o_ref.dtype)


def matmul_wres(a, w_t, layer, row0, n, out_dtype, bias=None, cast=None, tm=MM_TILE, tn=768):
    m, k = a.shape
    assert n % tn == 0 and m % tm == 0 and row0 % 8 == 0
    ni = m // tm
    in_specs = [pl.BlockSpec((tm, k), lambda j, i: (i, 0)),
                pl.BlockSpec((None, pl.Element(tn), pl.Element(k)),
                             lambda j, i: (layer, pl.multiple_of(row0 + j * tn, 8), 0))]
    args = [a, w_t]
    out_specs = [pl.BlockSpec((tm, tn), lambda j, i: (i, j))]
    out_shape = [jax.ShapeDtypeStruct((m, n), out_dtype)]
    if bias is not None:
        in_specs.append(pl.BlockSpec((1, tn), lambda j, i: (0, j)))
        args.append(bias.reshape(1, n))
    if cast is not None:
        assert cast.n_blocks <= (n // tn) * ni

        def step(j, i):
            return jnp.minimum(j * ni + i, cast.n_blocks - 1)

        in_specs.append(pl.BlockSpec(cast.block, lambda j, i: cast.plan(step(j, i))[0]))
        args.append(cast.src)
        out_specs.append(pl.BlockSpec(cast.block, lambda j, i: cast.plan(step(j, i))[1]))
        out_shape.append(jax.ShapeDtypeStruct(cast.out_shape, BF16))
    res = pl.pallas_call(
        functools.partial(_mm_wres_kernel, sigmoid=bias is not None,
                          cast_plan=None if cast is None else cast.plan,
                          cast_blocks=None if cast is None else cast.n_blocks),
        grid=(n // tn, ni),
        in_specs=in_specs,
        out_specs=out_specs,
        out_shape=out_shape,
        compiler_params=(_cparams(("parallel", "parallel"), 52) if cast is None
                         else _cparams(("arbitrary", "arbitrary"), 60)),
        name="mm_wres" if bias is None else "mm_wres_sigmoid",
    )(*args)
    return res if cast is not None else res[0]


def _mm_kacc_kernel(a_ref, b_ref, o_ref, *, k_last):
    kk = pl.program_id(1)
    last = pl.num_programs(1) - 1

    @pl.when(kk == 0)
    def _():
        o_ref[...] = jnp.zeros_like(o_ref)

    @pl.when(kk < last)
    def _():
        o_ref[...] += jnp.dot(a_ref[...], b_ref[...], preferred_element_type=F32)

    @pl.when(kk == last)
    def _():
        o_ref[...] += jnp.dot(a_ref[:, :k_last], b_ref[:k_last, :], preferred_element_type=F32)


def matmul_kacc(a, w, layer, k_live, tm=MM_TILE, tk=MM_TILE):
    m, k = a.shape
    n = w.shape[2]
    assert k % tk == 0 and 0 < k - k_live < tk
    return pl.pallas_call(
        functools.partial(_mm_kacc_kernel, k_last=k_live - (k - tk)),
        grid=(m // tm, k // tk),
        in_specs=[pl.BlockSpec((tm, tk), lambda i, kk: (i, kk)),
                  pl.BlockSpec((None, tk, n), lambda i, kk: (layer, kk, 0))],
        out_specs=pl.BlockSpec((tm, n), lambda i, kk: (i, 0)),
        out_shape=jax.ShapeDtypeStruct((m, n), F32),
        compiler_params=_cparams(("parallel", "arbitrary"), 56),
        name="mm_kacc",
    )(a, w)


def _conv3(p, w_ref):
    t = lax.broadcasted_iota(jnp.int32, p.shape, 0)
    prev = jnp.where(t == 0, 0.0, pltpu.roll(p, 1, 0))
    nxt = jnp.where(t == SEQ - 1, 0.0, pltpu.roll(p, SEQ - 1, 0))
    return prev * w_ref[0:1, :] + p * w_ref[1:2, :] + nxt * w_ref[2:3, :]


def _short_conv_kernel(b_ref, c_ref, h_ref, w_ref, o_ref):
    p = c_ref[...].astype(F32) * h_ref[...].astype(F32)
    o_ref[...] = (b_ref[...].astype(F32) * _conv3(p, w_ref)).astype(o_ref.dtype)


def short_conv_mixer(hm, w_conv, bsz, tc=256):
    nb = BRANCH_WIDTH // tc

    def spec(off):
        return pl.BlockSpec((SEQ, tc), lambda b, j: (b, off // tc + j))

    return pl.pallas_call(
        _short_conv_kernel,
        grid=(bsz, nb),
        in_specs=[spec(OFF_CONV_B), spec(OFF_CONV_C), spec(OFF_CONV_H),
                  pl.BlockSpec((3, tc), lambda b, j: (0, j))],
        out_specs=pl.BlockSpec((SEQ, tc), lambda b, j: (b, j)),
        out_shape=jax.ShapeDtypeStruct((bsz * SEQ, BRANCH_WIDTH), BF16),
        compiler_params=_cparams(("parallel", "parallel"), 48),
        name="short_conv",
    )(hm, hm, hm, w_conv)


def _ffn_in_kernel(xp_ref, x_ref, xn_ref, wa_ref, wu_ref, wc_ref, bc_ref, o_ref, xext_ref, *,
                   tiles_per_seq, n_last):
    i = pl.program_id(0)
    tm = x_ref.shape[0]
    n_ext = tm + 2 * FFN_HALO

    @pl.when(pl.program_id(1) == 0)
    def _():
        xext_ref[0:FFN_HALO] = xp_ref[...]
        xext_ref[FFN_HALO:FFN_HALO + tm] = x_ref[...]
        xext_ref[FFN_HALO + tm:n_ext] = xn_ref[...]

    def compute(ncols):
        cs = slice(0, ncols)
        a_ext = jnp.dot(xext_ref[...], wa_ref[:, cs], preferred_element_type=F32)
        u = jnp.dot(x_ref[...], wu_ref[:, cs], preferred_element_type=F32)
        a = a_ext[FFN_HALO:FFN_HALO + tm]
        prev = pltpu.roll(a_ext, 1, 0)[FFN_HALO:FFN_HALO + tm]
        nxt = pltpu.roll(a_ext, n_ext - 1, 0)[FFN_HALO:FFN_HALO + tm]
        pos = lax.rem(i, tiles_per_seq)
        row = lax.broadcasted_iota(jnp.int32, a.shape, 0)
        prev = jnp.where(row == jnp.where(pos == 0, 0, -1), 0.0, prev)
        nxt = jnp.where(row == jnp.where(pos == tiles_per_seq - 1, tm - 1, -1), 0.0, nxt)
        c = prev * wc_ref[0:1, cs] + a * wc_ref[1:2, cs] + nxt * wc_ref[2:3, cs] + bc_ref[:, cs]
        g = 0.5 * c * (1.0 + lax.erf(c * (2.0 ** -0.5)))
        o_ref[:, cs] = (g * u).astype(o_ref.dtype)
        if ncols < o_ref.shape[1]:
            o_ref[:, ncols:] = jnp.zeros((tm, o_ref.shape[1] - ncols), o_ref.dtype)

    j = pl.program_id(1)
    last = pl.num_programs(1) - 1
    pl.when(j < last)(lambda: compute(o_ref.shape[1]))
    pl.when(j == last)(lambda: compute(n_last))


def ffn_in(xb, w_ff, layer, w_conv, b_conv, n_live, tm=MM_TILE, tn=512):
    m, k = xb.shape
    nf = w_ff.shape[2] // 2
    nb = nf // tn
    hb = tm // FFN_HALO
    assert nf % tn == 0 and 0 < nf - n_live < tn
    return pl.pallas_call(
        functools.partial(_ffn_in_kernel, tiles_per_seq=SEQ // tm, n_last=n_live - (nf - tn)),
        grid=(m // tm, nb),
        in_specs=[pl.BlockSpec((FFN_HALO, k), lambda i, j: (jnp.maximum(i * hb - 1, 0), 0)),
                  pl.BlockSpec((tm, k), lambda i, j: (i, 0)),
                  pl.BlockSpec((FFN_HALO, k), lambda i, j: (jnp.minimum((i + 1) * hb, m // FFN_HALO - 1), 0)),
                  pl.BlockSpec((None, k, tn), lambda i, j: (layer, 0, j)),
                  pl.BlockSpec((None, k, tn), lambda i, j: (layer, 0, nb + j)),
                  pl.BlockSpec((3, tn), lambda i, j: (0, j)),
                  pl.BlockSpec((1, tn), lambda i, j: (0, j))],
        out_specs=pl.BlockSpec((tm, tn), lambda i, j: (i, j)),
        out_shape=jax.ShapeDtypeStruct((m, nf), BF16),
        scratch_shapes=[pltpu.VMEM((tm + 2 * FFN_HALO, k), BF16)],
        compiler_params=_cparams(("parallel", "arbitrary"), 52),
        name="ffn_in",
    )(xb, xb, xb, w_ff, w_ff, w_conv, b_conv)


def _na_key_start(band):
    lo = band * NA_QROWS - NA_WIN_ROWS // 2
    hi = ROWS - NA_KROWS
    if isinstance(band, int):
        return min(max(lo, 0), hi)
    return jnp.clip(lo, 0, hi)


NA_SCALE = NA_HEAD_DIM ** -0.5
LOG2_E = 1.4426950408889634
NA_ROW_OFFSETS = 2 * NA_WIN_ROWS - 1
NA_PAIR_CODES = 3 * NA_ROW_OFFSETS


def _na_pair_tables(rpb):
    qc = np.arange(GRID_W)[:, None]
    kc = np.arange(GRID_W)[None, :]
    wc = np.clip(qc - NA_WIN_COLS // 2, 0, GRID_W - NA_WIN_COLS)
    col_valid = (kc >= wc) & (kc < wc + NA_WIN_COLS)
    cidx = np.clip(kc - qc + NA_WIN_COLS - 1, 0, 2 * NA_WIN_COLS - 2)
    tm = jnp.where(col_valid[None, None], rpb[:, :, cidx] * (1.0 / NA_SCALE), NEG_INF)
    tm = jnp.concatenate([tm, jnp.full_like(tm[:, :1], NEG_INF)], axis=1)
    out = NA_ROW_OFFSETS
    both = np.arange(out - 1)
    one = np.arange(out)
    left = np.concatenate([both, one, np.full(out, out), [out]])
    right = np.concatenate([both + 1, np.full(out, out), one, [out]])
    return jnp.concatenate([tm[:, left], tm[:, right]], axis=-1)


def _na_pair_code(band, i, jp):
    r = band * NA_QROWS + i
    wr = jnp.clip(r - NA_WIN_ROWS // 2, 0, ROWS - NA_WIN_ROWS)
    kl = _na_key_start(band) + 2 * jp
    dl = kl - r + NA_WIN_ROWS - 1
    vl = jnp.logical_and(kl >= wr, kl < wr + NA_WIN_ROWS)
    vr = jnp.logical_and(kl + 1 >= wr, kl + 1 < wr + NA_WIN_ROWS)
    out = NA_ROW_OFFSETS
    return jnp.where(jnp.logical_and(vl, vr), dl,
                     jnp.where(vl, out - 1 + dl, jnp.where(vr, 2 * out - 1 + dl + 1, 3 * out - 1)))


def _na_kernel(q_ref, k_ref, v_ref, bias_ref, o_ref):
    band = pl.program_id(2)
    k0 = pl.multiple_of(_na_key_start(band) * GRID_W, GRID_W)
    nk = NA_KROWS * GRID_W
    codes = [[_na_pair_code(band, i, jp) for jp in range(NA_KROWS // 2)] for i in range(NA_QROWS)]
    for h in range(NA_HEADS_PER_STEP):
        cs = slice(h * NA_HEAD_DIM, (h + 1) * NA_HEAD_DIM)
        k = k_ref[pl.ds(k0, nk), cs]
        v = v_ref[pl.ds(k0, nk), cs]
        bias = jnp.concatenate([jnp.concatenate([bias_ref[h, c] for c in row], axis=1) for row in codes], axis=0)
        t = lax.dot_general(q_ref[:, cs], k, (((1,), (1,)), ((), ())), preferred_element_type=F32) + bias
        m = jnp.max(t, axis=-1, keepdims=True)
        p = jnp.exp2((t - m) * (NA_SCALE * LOG2_E))
        l = jnp.sum(p, axis=-1, keepdims=True)
        o = jnp.dot(p.astype(BF16), v, preferred_element_type=F32)
        o_ref[:, cs] = (o / l).astype(o_ref.dtype)


def neighbourhood_attention(hm, bias_tab, bsz):
    nbands = ROWS // NA_QROWS
    tq = NA_QROWS * GRID_W
    hp = NA_HEADS_PER_STEP
    wg = hp * NA_HEAD_DIM
    qo, ko, vo = (off // wg for off in (OFF_NA_Q, OFF_NA_K, OFF_NA_V))
    return pl.pallas_call(
        _na_kernel,
        grid=(NA_HEADS // hp, bsz, nbands),
        in_specs=[
            pl.BlockSpec((tq, wg), lambda g, b, r: (b * nbands + r, qo + g)),
            pl.BlockSpec((SEQ, wg), lambda g, b, r: (b, ko + g)),
            pl.BlockSpec((SEQ, wg), lambda g, b, r: (b, vo + g)),
            pl.BlockSpec((hp, NA_PAIR_CODES, GRID_W, 2 * GRID_W), lambda g, b, r: (g, 0, 0, 0)),
        ],
        out_specs=pl.BlockSpec((tq, wg), lambda g, b, r: (b * nbands + r, g)),
        out_shape=jax.ShapeDtypeStruct((bsz * SEQ, BRANCH_WIDTH), BF16),
        compiler_params=_cparams(("parallel", "parallel", "parallel"), 48),
        name="natten",
    )(hm, hm, hm, bias_tab)


def _gla_scan_kernel(qf_ref, kf_ref, vf_ref, af_ref, qb_ref, kb_ref, vb_ref, ab_ref,
                     wdec_ref, bdec_ref, of_ref, ob_ref, state_ref):
    @pl.when(pl.program_id(1) == 0)
    def _():
        state_ref[...] = jnp.zeros_like(state_ref)

    row = lax.broadcasted_iota(jnp.int32, (GLA_CHUNK, GLA_CHUNK), 0)
    col = lax.broadcasted_iota(jnp.int32, (GLA_CHUNK, GLA_CHUNK), 1)
    nt = (((1,), (1,)), ((), ()))
    tn = (((0,), (0,)), ((), ()))
    dirs = ((qf_ref, kf_ref, vf_ref, af_ref, of_ref, row >= col, GLA_CHUNK - 1),
            (qb_ref, kb_ref, vb_ref, ab_ref, ob_ref, row <= col, 0))
    for d, (q_ref, k_ref, v_ref, a_ref, o_ref, tri, last_row) in enumerate(dirs):
        z = jnp.dot(a_ref[...].astype(BF16), wdec_ref[d], preferred_element_type=F32) + bdec_ref[d:d + 1, :]
        log_a = (jnp.minimum(z, 0.0) - jnp.log(1.0 + jnp.exp(-jnp.abs(z)))) * (1.0 / GLA_GATE_TAU)
        cum = jnp.dot(tri.astype(F32), log_a, precision=lax.Precision.HIGHEST, preferred_element_type=F32)
        last = cum[last_row:last_row + 1, :]
        q_in = q_ref[...].astype(F32) * ((GLA_KEY_DIM ** -0.5) * jnp.exp(cum))
        kk = k_ref[...].astype(F32)
        k_in = kk * jnp.exp(-cum)
        k_out = kk * jnp.exp(last - cum)
        dec = jnp.exp(last)
        for h in range(GLA_HEADS):
            ks = slice(h * GLA_KEY_DIM, (h + 1) * GLA_KEY_DIM)
            vs = slice(h * GLA_VALUE_DIM, (h + 1) * GLA_VALUE_DIM)
            qh = q_in[:, ks].astype(BF16)
            vh = v_ref[:, vs]
            att = lax.dot_general(qh, k_in[:, ks].astype(BF16), nt, preferred_element_type=F32)
            att = jnp.where(tri, att, 0.0).astype(BF16)
            st = state_ref[d, h]
            o = jnp.dot(att, vh, preferred_element_type=F32)
            o = o + lax.dot_general(qh, st.astype(BF16), nt, preferred_element_type=F32)
            o_ref[:, vs] = o
            upd = lax.dot_general(vh, k_out[:, ks].astype(BF16), tn, preferred_element_type=F32)
            state_ref[d, h] = st * dec[:, ks] + upd


def gla_scan(hm, a_lr, wdec, bdec, bsz):
    qo = OFF_GLA_Q // GLA_KEY_WIDTH
    ko = OFF_GLA_K // GLA_KEY_WIDTH
    vo = OFF_GLA_V // BRANCH_WIDTH

    def fwd(colblk):
        return lambda b, c: (b * N_CHUNKS + c, colblk)

    def bwd(colblk):
        return lambda b, c: (b * N_CHUNKS + N_CHUNKS - 1 - c, colblk)

    def specs(idx):
        return [pl.BlockSpec((GLA_CHUNK, GLA_KEY_WIDTH), idx(qo)),
                pl.BlockSpec((GLA_CHUNK, GLA_KEY_WIDTH), idx(ko)),
                pl.BlockSpec((GLA_CHUNK, BRANCH_WIDTH), idx(vo)),
                pl.BlockSpec((GLA_CHUNK, LANES), idx(0))]

    out = jax.ShapeDtypeStruct((bsz * SEQ, BRANCH_WIDTH), F32)
    return pl.pallas_call(
        _gla_scan_kernel,
        grid=(bsz, N_CHUNKS),
        in_specs=specs(fwd) + specs(bwd) + [
            pl.BlockSpec((2, LANES, GLA_KEY_WIDTH), lambda b, c: (0, 0, 0)),
            pl.BlockSpec((2, GLA_KEY_WIDTH), lambda b, c: (0, 0))],
        out_specs=[pl.BlockSpec((GLA_CHUNK, BRANCH_WIDTH), fwd(0)),
                   pl.BlockSpec((GLA_CHUNK, BRANCH_WIDTH), bwd(0))],
        out_shape=[out, out],
        scratch_shapes=[pltpu.VMEM((2, GLA_HEADS, GLA_VALUE_DIM, GLA_KEY_DIM), F32)],
        compiler_params=_cparams(("parallel", "arbitrary"), 32),
        name="gla_scan",
    )(hm, hm, hm, a_lr, hm, hm, hm, a_lr, wdec, bdec)


def _gla_out_kernel(of_ref, ob_ref, r_ref, g_ref, o_ref):
    for h in range(GLA_HEADS):
        vs = slice(h * GLA_VALUE_DIM, (h + 1) * GLA_VALUE_DIM)
        o = of_ref[:, vs] + ob_ref[:, vs]
        o = o * lax.rsqrt(jnp.mean(o * o, axis=-1, keepdims=True) + RMS_EPS) * g_ref[:, vs]
        r = r_ref[:, vs].astype(F32)
        o_ref[:, vs] = (o * (r * jax.nn.sigmoid(r))).astype(o_ref.dtype)


def gla_output(o_f, o_b, hm, norm_g, tm=512):
    m = o_f.shape[0]
    row = pl.BlockSpec((tm, BRANCH_WIDTH), lambda i: (i, 0))
    return pl.pallas_call(
        _gla_out_kernel,
        grid=(m // tm,),
        in_specs=[row, row,
                  pl.BlockSpec((tm, BRANCH_WIDTH), lambda i: (i, OFF_GLA_R // BRANCH_WIDTH)),
                  pl.BlockSpec((1, BRANCH_WIDTH), lambda i: (0, 0))],
        out_specs=row,
        out_shape=jax.ShapeDtypeStruct((m, BRANCH_WIDTH), BF16),
        compiler_params=_cparams(("parallel",), 40),
        name="gla_output",
    )(o_f, o_b, hm, norm_g.reshape(1, BRANCH_WIDTH))


def _mix_kernel(y0_ref, y1_ref, y2_ref, w_ref, g0_ref, g1_ref, g2_ref, o_ref):
    acc = None
    for i, (y_ref, g_ref) in enumerate(((y0_ref, g0_ref), (y1_ref, g1_ref), (y2_ref, g2_ref))):
        t = g_ref[...].astype(F32) * jnp.dot(y_ref[...], w_ref[i], preferred_element_type=F32)
        acc = t if acc is None else acc + t
    o_ref[...] = acc.astype(o_ref.dtype)


def gated_mix(ys, w_branch, layer, gates, tm=MM_TILE, tn=512):
    m = ys[0].shape[0]
    nj = D_MODEL // tn
    yspec = pl.BlockSpec((tm, BRANCH_WIDTH), lambda i, j: (i, 0))

    def gspec(br):
        return pl.BlockSpec((tm, tn), lambda i, j: (i, br * nj + j))

    return pl.pallas_call(
        _mix_kernel,
        grid=(m // tm, nj),
        in_specs=[yspec, yspec, yspec,
                  pl.BlockSpec((None, N_BRANCH, BRANCH_WIDTH, tn), lambda i, j: (layer, 0, 0, j)),
                  gspec(0), gspec(1), gspec(2)],
        out_specs=pl.BlockSpec((tm, tn), lambda i, j: (i, j)),
        out_shape=jax.ShapeDtypeStruct((m, D_MODEL), BF16),
        compiler_params=_cparams(("parallel", "parallel"), 52),
        name="gated_mix",
    )(*ys, w_branch, gates, gates, gates)


def _layer_norm(y, g_ref, b_ref):
    mu = jnp.mean(y, axis=-1, keepdims=True)
    yc = y - mu
    var = jnp.mean(yc * yc, axis=-1, keepdims=True)
    return yc * lax.rsqrt(var + LN_EPS) * g_ref[...] + b_ref[...]


def _ln1_kernel(s_ref, g_ref, b_ref, ob_ref):
    ob_ref[...] = _layer_norm(s_ref[...], g_ref, b_ref).astype(BF16)


def _ln2_kernel(s1_ref, s2_ref, g1_ref, b1_ref, g2_ref, b2_ref, of_ref, ob_ref, *, alpha):
    x1 = _layer_norm(s1_ref[...], g1_ref, b1_ref)
    out = _layer_norm(alpha * x1 + s2_ref[...], g2_ref, b2_ref)
    of_ref[...] = out
    ob_ref[...] = out.astype(BF16)


LN_ROWS = 256


def layer_norm_bf16(s1, g1, b1):
    m, d = s1.shape
    row = pl.BlockSpec((LN_ROWS, d), lambda i: (i, 0))
    vec = pl.BlockSpec((1, d), lambda i: (0, 0))
    return pl.pallas_call(
        _ln1_kernel,
        grid=(m // LN_ROWS,),
        in_specs=[row, vec, vec],
        out_specs=row,
        out_shape=jax.ShapeDtypeStruct((m, d), BF16),
        compiler_params=_cparams(("parallel",), 40),
        name="ln1",
    )(s1, g1.reshape(1, d), b1.reshape(1, d))


def layer_norm_ffn_residual(s1, s2, g1, b1, g2, b2, alpha):
    m, d = s1.shape
    row = pl.BlockSpec((LN_ROWS, d), lambda i: (i, 0))
    vec = pl.BlockSpec((1, d), lambda i: (0, 0))
    return pl.pallas_call(
        functools.partial(_ln2_kernel, alpha=alpha),
        grid=(m // LN_ROWS,),
        in_specs=[row, row, vec, vec, vec, vec],
        out_specs=[row, row],
        out_shape=[jax.ShapeDtypeStruct((m, d), F32), jax.ShapeDtypeStruct((m, d), BF16)],
        compiler_params=_cparams(("parallel",), 40),
        name="ln2",
    )(s1, s2, g1.reshape(1, d), b1.reshape(1, d), g2.reshape(1, d), b2.reshape(1, d))


def kernel(x, w_in, b_gate, w_short_conv, na_rpb, gla_w_decay, gla_b_decay, gla_norm_g, w_branch, w_out,
           ln1_g, ln1_b, w_ffn_in, w_ffn_conv, b_ffn_conv, w_ffn_out, ln2_g, ln2_b):
    bsz, s, d = x.shape
    depth = w_in.shape[0]
    assert (s, d) == (SEQ, D_MODEL) and w_in.shape[2] == OFF_GATE + N_BRANCH * D_MODEL
    assert w_ffn_in.shape[2] == 2 * D_FF
    alpha = (2.0 * depth) ** 0.25
    ff_pad = D_FF_PAD - D_FF

    w_in_t = jnp.swapaxes(w_in, 1, 2)
    w_br_b = w_branch.astype(BF16)
    w_out_b = w_out.astype(BF16)
    w_conv_ff = jnp.pad(w_ffn_conv, ((0, 0), (0, 0), (0, ff_pad)))
    b_conv_ff = jnp.pad(b_ffn_conv, ((0, 0), (0, ff_pad)))

    xf = x.reshape(bsz * s, d)
    xb = xf.astype(BF16)
    for l in range(depth):
        hm = matmul_wres(xb, w_in_t, l, 0, MAIN_WIDTH, BF16, cast=ffn_in_padcast(w_ffn_in) if l == 0 else None)
        gates = matmul_wres(xb, w_in_t, l, OFF_GATE, N_BRANCH * D_MODEL, BF16, bias=b_gate[l],
                            cast=ffn_out_padcast(w_ffn_out, rb=LANES) if l == 0 else None)
        if l == 0:
            (hm, w_ff_b), (gates, w_ffo_b) = hm, gates
        a_lr = matmul_wres(xb, w_in_t, l, MAIN_WIDTH, LANES, F32, tn=LANES)

        y_conv = short_conv_mixer(hm, w_short_conv[l], bsz)
        y_na = neighbourhood_attention(hm, _na_pair_tables(na_rpb[l]), bsz)
        wdec = jnp.zeros((2, LANES, GLA_KEY_WIDTH), F32)
        wdec = wdec.at[0, :GLA_GATE_RANK].set(gla_w_decay[l, 0])
        wdec = wdec.at[1, GLA_GATE_RANK:2 * GLA_GATE_RANK].set(gla_w_decay[l, 1])
        o_f, o_b = gla_scan(hm, a_lr, wdec.astype(BF16), gla_b_decay[l], bsz)
        y_gla = gla_output(o_f, o_b, hm, gla_norm_g[l])

        mixed = gated_mix((y_conv, y_na, y_gla), w_br_b, l, gates)
        s1 = matmul_residual(mixed, w_out_b, l, xf, alpha)
        xb = layer_norm_bf16(s1, ln1_g[l], ln1_b[l])

        g = ffn_in(xb, w_ff_b, l, w_conv_ff[l], b_conv_ff[l].reshape(1, D_FF_PAD), D_FF)
        s2 = matmul_kacc(g, w_ffo_b, l, D_FF)
        xf, xb = layer_norm_ffn_residual(s1, s2, ln1_g[l], ln1_b[l], ln2_g[l], ln2_b[l], alpha)
    return xf.reshape(bsz, s, d)
```

```python
import functools
from typing import Callable, NamedTuple

import numpy as np
import jax
import jax.numpy as jnp
from jax import lax
from jax.experimental import pallas as pl
from jax.experimental.pallas import tpu as pltpu

F32 = jnp.float32
BF16 = jnp.bfloat16

D_MODEL = 4096
SEQ = 4096
GRID_W = 64
ROWS = SEQ // GRID_W
BRANCH_WIDTH = D_MODEL // 2
N_BRANCH = 3
NA_HEAD_DIM = 128
NA_HEADS = BRANCH_WIDTH // NA_HEAD_DIM
NA_WIN_ROWS = 8
NA_WIN_COLS = 16
NA_QROWS = 8
NA_KROWS = 16
NA_HEADS_PER_STEP = 4
GLA_HEADS = 4
GLA_KEY_WIDTH = BRANCH_WIDTH // 2
GLA_KEY_DIM = GLA_KEY_WIDTH // GLA_HEADS
GLA_VALUE_DIM = BRANCH_WIDTH // GLA_HEADS
GLA_GATE_RANK = 16
GLA_GATE_TAU = 16.0
GLA_CHUNK = 64
N_CHUNKS = SEQ // GLA_CHUNK
D_FF = 256 * ((8 * D_MODEL // 3 + 255) // 256)
LN_EPS = 1e-5
RMS_EPS = 1e-6
NEG_INF = -1e30

OFF_CONV_B = 0
OFF_CONV_C = BRANCH_WIDTH
OFF_CONV_H = 2 * BRANCH_WIDTH
OFF_NA_Q = 3 * BRANCH_WIDTH
OFF_NA_K = 4 * BRANCH_WIDTH
OFF_NA_V = 5 * BRANCH_WIDTH
OFF_GLA_Q = 6 * BRANCH_WIDTH
OFF_GLA_K = OFF_GLA_Q + GLA_KEY_WIDTH
OFF_GLA_V = OFF_GLA_K + GLA_KEY_WIDTH
OFF_GLA_R = OFF_GLA_V + BRANCH_WIDTH
MAIN_WIDTH = OFF_GLA_R + BRANCH_WIDTH
OFF_GATE = MAIN_WIDTH + 2 * GLA_GATE_RANK

LANES = 128
V7X_VMEM_BYTES = 64 * 1024 * 1024
MM_TILE = 1024
D_FF_PAD = MM_TILE * (-(-D_FF // MM_TILE))
FFN_HALO = 16
MIB = 1024 * 1024


def _cparams(semantics, vmem_mib):
    assert vmem_mib * MIB < V7X_VMEM_BYTES
    return pltpu.CompilerParams(dimension_semantics=semantics, vmem_limit_bytes=vmem_mib * MIB)


_NT = (((1,), (1,)), ((), ()))


def _mm_residual_kernel(a_ref, b_ref, res_ref, o_ref, *, alpha):
    acc = jnp.dot(a_ref[...], b_ref[...], preferred_element_type=F32)
    o_ref[...] = alpha * res_ref[...] + acc


def matmul_residual(a, w, layer, residual, alpha, tm=MM_TILE, tn=512):
    m, k = a.shape
    n = w.shape[2]
    assert n % tn == 0 and m % tm == 0
    return pl.pallas_call(
        functools.partial(_mm_residual_kernel, alpha=alpha),
        grid=(m // tm, n // tn),
        in_specs=[pl.BlockSpec((tm, k), lambda i, j: (i, 0)),
                  pl.BlockSpec((None, k, tn), lambda i, j: (layer, 0, j)),
                  pl.BlockSpec((tm, tn), lambda i, j: (i, j))],
        out_specs=pl.BlockSpec((tm, tn), lambda i, j: (i, j)),
        out_shape=jax.ShapeDtypeStruct((m, n), F32),
        compiler_params=_cparams(("parallel", "parallel"), 40),
        name="mm_residual",
    )(a, w, residual)


class PadCast(NamedTuple):
    src: jax.Array
    out_shape: tuple
    block: tuple
    n_blocks: int
    plan: Callable


def ffn_in_padcast(w_ffn_in, cb=LANES):
    depth, d, _ = w_ffn_in.shape
    live, padded = D_FF // cb, D_FF_PAD // cb

    def plan(t):
        layer, c = t // (2 * padded), t % (2 * padded)
        half, jj = c // padded, c % padded
        return (layer, 0, half * live + jnp.minimum(jj, live - 1)), (layer, 0, c)

    return PadCast(w_ffn_in, (depth, d, 2 * D_FF_PAD), (None, d, cb), depth * 2 * padded, plan)


def ffn_out_padcast(w_ffn_out, rb=LANES):
    depth, _, d = w_ffn_out.shape
    live, padded = D_FF // rb, D_FF_PAD // rb

    def plan(t):
        layer, r = t // padded, t % padded
        return (layer, jnp.minimum(r, live - 1), 0), (layer, r, 0)

    return PadCast(w_ffn_out, (depth, D_FF_PAD, d), (None, rb, d), depth * padded, plan)


def _mm_wres_kernel(*refs, sigmoid, carries_cast):
    a_ref, w_ref = refs[:2]
    refs = refs[2:]
    acc = lax.dot_general(a_ref[...], w_ref[...].astype(BF16), _NT, preferred_element_type=F32)
    if sigmoid:
        acc = jax.nn.sigmoid(acc + refs[0][...])
        refs = refs[1:]
    if carries_cast:
        src_ref, o_ref, dst_ref = refs
        dst_ref[...] = src_ref[...].astype(dst_ref.dtype)
    else:
        (o_ref,) = refs
    o_ref[...] = acc.astype(o_ref.dtype)


def matmul_wres(a, w_t, layer, row0, n, out_dtype, bias=None, cast=None, tm=MM_TILE, tn=768):
    m, k = a.shape
    assert n % tn == 0 and m % tm == 0 and row0 % 8 == 0
    ni = m // tm
    in_specs = [pl.BlockSpec((tm, k), lambda j, i: (i, 0)),
                pl.BlockSpec((None, pl.Element(tn), pl.Element(k)),
                             lambda j, i: (layer, pl.multiple_of(row0 + j * tn, 8), 0))]
    args = [a, w_t]
    out_specs = [pl.BlockSpec((tm, tn), lambda j, i: (i, j))]
    out_shape = [jax.ShapeDtypeStruct((m, n), out_dtype)]
    if bias is not None:
        in_specs.append(pl.BlockSpec((1, tn), lambda j, i: (0, j)))
        args.append(bias.reshape(1, n))
    if cast is not None:
        assert cast.n_blocks <= (n // tn) * ni

        def step(j, i):
            return jnp.minimum(j * ni + i, cast.n_blocks - 1)

        in_specs.append(pl.BlockSpec(cast.block, lambda j, i: cast.plan(step(j, i))[0]))
        args.append(cast.src)
        out_specs.append(pl.BlockSpec(cast.block, lambda j, i: cast.plan(step(j, i))[1]))
        out_shape.append(jax.ShapeDtypeStruct(cast.out_shape, BF16))
    res = pl.pallas_call(
        functools.partial(_mm_wres_kernel, sigmoid=bias is not None, carries_cast=cast is not None),
        grid=(n // tn, ni),
        in_specs=in_specs,
        out_specs=out_specs,
        out_shape=out_shape,
        compiler_params=(_cparams(("parallel", "parallel"), 52) if cast is None
                         else _cparams(("arbitrary", "arbitrary"), 60)),
        name="mm_wres" if bias is None else "mm_wres_sigmoid",
    )(*args)
    return res if cast is not None else res[0]


def _mm_kacc_kernel(a_ref, b_ref, o_ref, *, k_last):
    kk = pl.program_id(1)
    last = pl.num_programs(1) - 1

    @pl.when(kk == 0)
    def _():
        o_ref[...] = jnp.zeros_like(o_ref)

    @pl.when(kk < last)
    def _():
        o_ref[...] += jnp.dot(a_ref[...], b_ref[...], preferred_element_type=F32)

    @pl.when(kk == last)
    def _():
        o_ref[...] += jnp.dot(a_ref[:, :k_last], b_ref[:k_last, :], preferred_element_type=F32)


def matmul_kacc(a, w, layer, k_live, tm=MM_TILE, tk=MM_TILE):
    m, k = a.shape
    n = w.shape[2]
    assert k % tk == 0 and 0 < k - k_live < tk
    return pl.pallas_call(
        functools.partial(_mm_kacc_kernel, k_last=k_live - (k - tk)),
        grid=(m // tm, k // tk),
        in_specs=[pl.BlockSpec((tm, tk), lambda i, kk: (i, kk)),
                  pl.BlockSpec((None, tk, n), lambda i, kk: (layer, kk, 0))],
        out_specs=pl.BlockSpec((tm, n), lambda i, kk: (i, 0)),
        out_shape=jax.ShapeDtypeStruct((m, n), F32),
        compiler_params=_cparams(("parallel", "arbitrary"), 56),
        name="mm_kacc",
    )(a, w)


def _conv3(p, w_ref):
    t = lax.broadcasted_iota(jnp.int32, p.shape, 0)
    prev = jnp.where(t == 0, 0.0, pltpu.roll(p, 1, 0))
    nxt = jnp.where(t == SEQ - 1, 0.0, pltpu.roll(p, SEQ - 1, 0))
    return prev * w_ref[0:1, :] + p * w_ref[1:2, :] + nxt * w_ref[2:3, :]


def _short_conv_kernel(b_ref, c_ref, h_ref, w_ref, o_ref):
    p = c_ref[...].astype(F32) * h_ref[...].astype(F32)
    o_ref[...] = (b_ref[...].astype(F32) * _conv3(p, w_ref)).astype(o_ref.dtype)


def short_conv_mixer(hm, w_conv, bsz, tc=256):
    nb = BRANCH_WIDTH // tc

    def spec(off):
        return pl.BlockSpec((SEQ, tc), lambda b, j: (b, off // tc + j))

    return pl.pallas_call(
        _short_conv_kernel,
        grid=(bsz, nb),
        in_specs=[spec(OFF_CONV_B), spec(OFF_CONV_C), spec(OFF_CONV_H),
                  pl.BlockSpec((3, tc), lambda b, j: (0, j))],
        out_specs=pl.BlockSpec((SEQ, tc), lambda b, j: (b, j)),
        out_shape=jax.ShapeDtypeStruct((bsz * SEQ, BRANCH_WIDTH), BF16),
        compiler_params=_cparams(("parallel", "parallel"), 48),
        name="short_conv",
    )(hm, hm, hm, w_conv)


def _ffn_in_kernel(xp_ref, x_ref, xn_ref, wa_ref, wu_ref, wc_ref, bc_ref, o_ref, xext_ref, *,
                   tiles_per_seq, n_last):
    i = pl.program_id(0)
    tm = x_ref.shape[0]
    n_ext = tm + 2 * FFN_HALO

    @pl.when(pl.program_id(1) == 0)
    def _():
        xext_ref[0:FFN_HALO] = xp_ref[...]
        xext_ref[FFN_HALO:FFN_HALO + tm] = x_ref[...]
        xext_ref[FFN_HALO + tm:n_ext] = xn_ref[...]

    def compute(ncols):
        cs = slice(0, ncols)
        a_ext = jnp.dot(xext_ref[...], wa_ref[:, cs], preferred_element_type=F32)
        u = jnp.dot(x_ref[...], wu_ref[:, cs], preferred_element_type=F32)
        a = a_ext[FFN_HALO:FFN_HALO + tm]
        prev = pltpu.roll(a_ext, 1, 0)[FFN_HALO:FFN_HALO + tm]
        nxt = pltpu.roll(a_ext, n_ext - 1, 0)[FFN_HALO:FFN_HALO + tm]
        pos = lax.rem(i, tiles_per_seq)
        row = lax.broadcasted_iota(jnp.int32, a.shape, 0)
        prev = jnp.where(row == jnp.where(pos == 0, 0, -1), 0.0, prev)
        nxt = jnp.where(row == jnp.where(pos == tiles_per_seq - 1, tm - 1, -1), 0.0, nxt)
        c = prev * wc_ref[0:1, cs] + a * wc_ref[1:2, cs] + nxt * wc_ref[2:3, cs] + bc_ref[:, cs]
        g = 0.5 * c * (1.0 + lax.erf(c * (2.0 ** -0.5)))
        o_ref[:, cs] = (g * u).astype(o_ref.dtype)
        if ncols < o_ref.shape[1]:
            o_ref[:, ncols:] = jnp.zeros((tm, o_ref.shape[1] - ncols), o_ref.dtype)

    j = pl.program_id(1)
    last = pl.num_programs(1) - 1
    pl.when(j < last)(lambda: compute(o_ref.shape[1]))
    pl.when(j == last)(lambda: compute(n_last))


def ffn_in(xb, w_ff, layer, w_conv, b_conv, n_live, tm=MM_TILE, tn=512):
    m, k = xb.shape
    nf = w_ff.shape[2] // 2
    nb = nf // tn
    hb = tm // FFN_HALO
    assert nf % tn == 0 and 0 < nf - n_live < tn
    return pl.pallas_call(
        functools.partial(_ffn_in_kernel, tiles_per_seq=SEQ // tm, n_last=n_live - (nf - tn)),
        grid=(m // tm, nb),
        in_specs=[pl.BlockSpec((FFN_HALO, k), lambda i, j: (jnp.maximum(i * hb - 1, 0), 0)),
                  pl.BlockSpec((tm, k), lambda i, j: (i, 0)),
                  pl.BlockSpec((FFN_HALO, k), lambda i, j: (jnp.minimum((i + 1) * hb, m // FFN_HALO - 1), 0)),
                  pl.BlockSpec((None, k, tn), lambda i, j: (layer, 0, j)),
                  pl.BlockSpec((None, k, tn), lambda i, j: (layer, 0, nb + j)),
                  pl.BlockSpec((3, tn), lambda i, j: (0, j)),
                  pl.BlockSpec((1, tn), lambda i, j: (0, j))],
        out_specs=pl.BlockSpec((tm, tn), lambda i, j: (i, j)),
        out_shape=jax.ShapeDtypeStruct((m, nf), BF16),
        scratch_shapes=[pltpu.VMEM((tm + 2 * FFN_HALO, k), BF16)],
        compiler_params=_cparams(("parallel", "arbitrary"), 52),
        name="ffn_in",
    )(xb, xb, xb, w_ff, w_ff, w_conv, b_conv)


def _na_key_start(band):
    lo = band * NA_QROWS - NA_WIN_ROWS // 2
    hi = ROWS - NA_KROWS
    if isinstance(band, int):
        return min(max(lo, 0), hi)
    return jnp.clip(lo, 0, hi)


NA_SCALE = NA_HEAD_DIM ** -0.5
LOG2_E = 1.4426950408889634
NA_ROW_OFFSETS = 2 * NA_WIN_ROWS - 1
NA_PAIR_CODES = 3 * NA_ROW_OFFSETS


def _na_pair_tables(rpb):
    qc = np.arange(GRID_W)[:, None]
    kc = np.arange(GRID_W)[None, :]
    wc = np.clip(qc - NA_WIN_COLS // 2, 0, GRID_W - NA_WIN_COLS)
    col_valid = (kc >= wc) & (kc < wc + NA_WIN_COLS)
    cidx = np.clip(kc - qc + NA_WIN_COLS - 1, 0, 2 * NA_WIN_COLS - 2)
    tm = jnp.where(col_valid[None, None], rpb[:, :, cidx] * (1.0 / NA_SCALE), NEG_INF)
    tm = jnp.concatenate([tm, jnp.full_like(tm[:, :1], NEG_INF)], axis=1)
    out = NA_ROW_OFFSETS
    both = np.arange(out - 1)
    one = np.arange(out)
    left = np.concatenate([both, one, np.full(out, out), [out]])
    right = np.concatenate([both + 1, np.full(out, out), one, [out]])
    return jnp.concatenate([tm[:, left], tm[:, right]], axis=-1)


def _na_pair_code(band, i, jp):
    r = band * NA_QROWS + i
    wr = jnp.clip(r - NA_WIN_ROWS // 2, 0, ROWS - NA_WIN_ROWS)
    kl = _na_key_start(band) + 2 * jp
    dl = kl - r + NA_WIN_ROWS - 1
    vl = jnp.logical_and(kl >= wr, kl < wr + NA_WIN_ROWS)
    vr = jnp.logical_and(kl + 1 >= wr, kl + 1 < wr + NA_WIN_ROWS)
    out = NA_ROW_OFFSETS
    return jnp.where(jnp.logical_and(vl, vr), dl,
                     jnp.where(vl, out - 1 + dl, jnp.where(vr, 2 * out - 1 + dl + 1, 3 * out - 1)))


def _na_kernel(q_ref, k_ref, v_ref, bias_ref, o_ref):
    band = pl.program_id(2)
    k0 = pl.multiple_of(_na_key_start(band) * GRID_W, GRID_W)
    nk = NA_KROWS * GRID_W
    codes = [[_na_pair_code(band, i, jp) for jp in range(NA_KROWS // 2)] for i in range(NA_QROWS)]
    for h in range(NA_HEADS_PER_STEP):
        cs = slice(h * NA_HEAD_DIM, (h + 1) * NA_HEAD_DIM)
        k = k_ref[pl.ds(k0, nk), cs]
        v = v_ref[pl.ds(k0, nk), cs]
        bias = jnp.concatenate([jnp.concatenate([bias_ref[h, c] for c in row], axis=1) for row in codes], axis=0)
        t = lax.dot_general(q_ref[:, cs], k, (((1,), (1,)), ((), ())), preferred_element_type=F32) + bias
        m = jnp.max(t, axis=-1, keepdims=True)
        p = jnp.exp2((t - m) * (NA_SCALE * LOG2_E))
        l = jnp.sum(p, axis=-1, keepdims=True)
        o = jnp.dot(p.astype(BF16), v, preferred_element_type=F32)
        o_ref[:, cs] = (o / l).astype(o_ref.dtype)


def neighbourhood_attention(hm, bias_tab, bsz):
    nbands = ROWS // NA_QROWS
    tq = NA_QROWS * GRID_W
    hp = NA_HEADS_PER_STEP
    wg = hp * NA_HEAD_DIM
    qo, ko, vo = (off // wg for off in (OFF_NA_Q, OFF_NA_K, OFF_NA_V))
    return pl.pallas_call(
        _na_kernel,
        grid=(NA_HEADS // hp, bsz, nbands),
        in_specs=[
            pl.BlockSpec((tq, wg), lambda g, b, r: (b * nbands + r, qo + g)),
            pl.BlockSpec((SEQ, wg), lambda g, b, r: (b, ko + g)),
            pl.BlockSpec((SEQ, wg), lambda g, b, r: (b, vo + g)),
            pl.BlockSpec((hp, NA_PAIR_CODES, GRID_W, 2 * GRID_W), lambda g, b, r: (g, 0, 0, 0)),
        ],
        out_specs=pl.BlockSpec((tq, wg), lambda g, b, r: (b * nbands + r, g)),
        out_shape=jax.ShapeDtypeStruct((bsz * SEQ, BRANCH_WIDTH), BF16),
        compiler_params=_cparams(("parallel", "parallel", "parallel"), 48),
        name="natten",
    )(hm, hm, hm, bias_tab)


def _gla_scan_kernel(qf_ref, kf_ref, vf_ref, af_ref, qb_ref, kb_ref, vb_ref, ab_ref,
                     wdec_ref, bdec_ref, of_ref, ob_ref, state_ref):
    @pl.when(pl.program_id(1) == 0)
    def _():
        state_ref[...] = jnp.zeros_like(state_ref)

    row = lax.broadcasted_iota(jnp.int32, (GLA_CHUNK, GLA_CHUNK), 0)
    col = lax.broadcasted_iota(jnp.int32, (GLA_CHUNK, GLA_CHUNK), 1)
    nt = (((1,), (1,)), ((), ()))
    tn = (((0,), (0,)), ((), ()))
    dirs = ((qf_ref, kf_ref, vf_ref, af_ref, of_ref, row >= col, GLA_CHUNK - 1),
            (qb_ref, kb_ref, vb_ref, ab_ref, ob_ref, row <= col, 0))
    for d, (q_ref, k_ref, v_ref, a_ref, o_ref, tri, last_row) in enumerate(dirs):
        z = jnp.dot(a_ref[...].astype(BF16), wdec_ref[d], preferred_element_type=F32) + bdec_ref[d:d + 1, :]
        log_a = (jnp.minimum(z, 0.0) - jnp.log(1.0 + jnp.exp(-jnp.abs(z)))) * (1.0 / GLA_GATE_TAU)
        cum = jnp.dot(tri.astype(F32), log_a, precision=lax.Precision.HIGHEST, preferred_element_type=F32)
        last = cum[last_row:last_row + 1, :]
        q_in = q_ref[...].astype(F32) * ((GLA_KEY_DIM ** -0.5) * jnp.exp(cum))
        kk = k_ref[...].astype(F32)
        k_in = kk * jnp.exp(-cum)
        k_out = kk * jnp.exp(last - cum)
        dec = jnp.exp(last)
        for h in range(GLA_HEADS):
            ks = slice(h * GLA_KEY_DIM, (h + 1) * GLA_KEY_DIM)
            vs = slice(h * GLA_VALUE_DIM, (h + 1) * GLA_VALUE_DIM)
            qh = q_in[:, ks].astype(BF16)
            vh = v_ref[:, vs]
            att = lax.dot_general(qh, k_in[:, ks].astype(BF16), nt, preferred_element_type=F32)
            att = jnp.where(tri, att, 0.0).astype(BF16)
            st = state_ref[d, h]
            o = jnp.dot(att, vh, preferred_element_type=F32)
            o = o + lax.dot_general(qh, st.astype(BF16), nt, preferred_element_type=F32)
            o_ref[:, vs] = o
            upd = lax.dot_general(vh, k_out[:, ks].astype(BF16), tn, preferred_element_type=F32)
            state_ref[d, h] = st * dec[:, ks] + upd


def gla_scan(hm, a_lr, wdec, bdec, bsz):
    qo = OFF_GLA_Q // GLA_KEY_WIDTH
    ko = OFF_GLA_K // GLA_KEY_WIDTH
    vo = OFF_GLA_V // BRANCH_WIDTH

    def fwd(colblk):
        return lambda b, c: (b * N_CHUNKS + c, colblk)

    def bwd(colblk):
        return lambda b, c: (b * N_CHUNKS + N_CHUNKS - 1 - c, colblk)

    def specs(idx):
        return [pl.BlockSpec((GLA_CHUNK, GLA_KEY_WIDTH), idx(qo)),
                pl.BlockSpec((GLA_CHUNK, GLA_KEY_WIDTH), idx(ko)),
                pl.BlockSpec((GLA_CHUNK, BRANCH_WIDTH), idx(vo)),
                pl.BlockSpec((GLA_CHUNK, LANES), idx(0))]

    out = jax.ShapeDtypeStruct((bsz * SEQ, BRANCH_WIDTH), F32)
    return pl.pallas_call(
        _gla_scan_kernel,
        grid=(bsz, N_CHUNKS),
        in_specs=specs(fwd) + specs(bwd) + [
            pl.BlockSpec((2, LANES, GLA_KEY_WIDTH), lambda b, c: (0, 0, 0)),
            pl.BlockSpec((2, GLA_KEY_WIDTH), lambda b, c: (0, 0))],
        out_specs=[pl.BlockSpec((GLA_CHUNK, BRANCH_WIDTH), fwd(0)),
                   pl.BlockSpec((GLA_CHUNK, BRANCH_WIDTH), bwd(0))],
        out_shape=[out, out],
        scratch_shapes=[pltpu.VMEM((2, GLA_HEADS, GLA_VALUE_DIM, GLA_KEY_DIM), F32)],
        compiler_params=_cparams(("parallel", "arbitrary"), 32),
        name="gla_scan",
    )(hm, hm, hm, a_lr, hm, hm, hm, a_lr, wdec, bdec)


def _gla_out_kernel(of_ref, ob_ref, r_ref, g_ref, o_ref):
    for h in range(GLA_HEADS):
        vs = slice(h * GLA_VALUE_DIM, (h + 1) * GLA_VALUE_DIM)
        o = of_ref[:, vs] + ob_ref[:, vs]
        o = o * lax.rsqrt(jnp.mean(o * o, axis=-1, keepdims=True) + RMS_EPS) * g_ref[:, vs]
        r = r_ref[:, vs].astype(F32)
        o_ref[:, vs] = (o * (r * jax.nn.sigmoid(r))).astype(o_ref.dtype)


def gla_output(o_f, o_b, hm, norm_g, tm=512):
    m = o_f.shape[0]
    row = pl.BlockSpec((tm, BRANCH_WIDTH), lambda i: (i, 0))
    return pl.pallas_call(
        _gla_out_kernel,
        grid=(m // tm,),
        in_specs=[row, row,
                  pl.BlockSpec((tm, BRANCH_WIDTH), lambda i: (i, OFF_GLA_R // BRANCH_WIDTH)),
                  pl.BlockSpec((1, BRANCH_WIDTH), lambda i: (0, 0))],
        out_specs=row,
        out_shape=jax.ShapeDtypeStruct((m, BRANCH_WIDTH), BF16),
        compiler_params=_cparams(("parallel",), 40),
        name="gla_output",
    )(o_f, o_b, hm, norm_g.reshape(1, BRANCH_WIDTH))


def _mix_kernel(y0_ref, y1_ref, y2_ref, w_ref, g0_ref, g1_ref, g2_ref, o_ref):
    acc = None
    for i, (y_ref, g_ref) in enumerate(((y0_ref, g0_ref), (y1_ref, g1_ref), (y2_ref, g2_ref))):
        t = g_ref[...].astype(F32) * jnp.dot(y_ref[...], w_ref[i], preferred_element_type=F32)
        acc = t if acc is None else acc + t
    o_ref[...] = acc.astype(o_ref.dtype)


def gated_mix(ys, w_branch, layer, gates, tm=MM_TILE, tn=512):
    m = ys[0].shape[0]
    nj = D_MODEL // tn
    yspec = pl.BlockSpec((tm, BRANCH_WIDTH), lambda i, j: (i, 0))

    def gspec(br):
        return pl.BlockSpec((tm, tn), lambda i, j: (i, br * nj + j))

    return pl.pallas_call(
        _mix_kernel,
        grid=(m // tm, nj),
        in_specs=[yspec, yspec, yspec,
                  pl.BlockSpec((None, N_BRANCH, BRANCH_WIDTH, tn), lambda i, j: (layer, 0, 0, j)),
                  gspec(0), gspec(1), gspec(2)],
        out_specs=pl.BlockSpec((tm, tn), lambda i, j: (i, j)),
        out_shape=jax.ShapeDtypeStruct((m, D_MODEL), BF16),
        compiler_params=_cparams(("parallel", "parallel"), 52),
        name="gated_mix",
    )(*ys, w_branch, gates, gates, gates)


def _layer_norm(y, g_ref, b_ref):
    mu = jnp.mean(y, axis=-1, keepdims=True)
    yc = y - mu
    var = jnp.mean(yc * yc, axis=-1, keepdims=True)
    return yc * lax.rsqrt(var + LN_EPS) * g_ref[...] + b_ref[...]


def _ln1_kernel(s_ref, g_ref, b_ref, ob_ref):
    ob_ref[...] = _layer_norm(s_ref[...], g_ref, b_ref).astype(BF16)


def _ln2_kernel(s1_ref, s2_ref, g1_ref, b1_ref, g2_ref, b2_ref, of_ref, ob_ref, *, alpha):
    x1 = _layer_norm(s1_ref[...], g1_ref, b1_ref)
    out = _layer_norm(alpha * x1 + s2_ref[...], g2_ref, b2_ref)
    of_ref[...] = out
    ob_ref[...] = out.astype(BF16)


LN_ROWS = 256


def layer_norm_bf16(s1, g1, b1):
    m, d = s1.shape
    row = pl.BlockSpec((LN_ROWS, d), lambda i: (i, 0))
    vec = pl.BlockSpec((1, d), lambda i: (0, 0))
    return pl.pallas_call(
        _ln1_kernel,
        grid=(m // LN_ROWS,),
        in_specs=[row, vec, vec],
        out_specs=row,
        out_shape=jax.ShapeDtypeStruct((m, d), BF16),
        compiler_params=_cparams(("parallel",), 40),
        name="ln1",
    )(s1, g1.reshape(1, d), b1.reshape(1, d))


def layer_norm_ffn_residual(s1, s2, g1, b1, g2, b2, alpha):
    m, d = s1.shape
    row = pl.BlockSpec((LN_ROWS, d), lambda i: (i, 0))
    vec = pl.BlockSpec((1, d), lambda i: (0, 0))
    return pl.pallas_call(
        functools.partial(_ln2_kernel, alpha=alpha),
        grid=(m // LN_ROWS,),
        in_specs=[row, row, vec, vec, vec, vec],
        out_specs=[row, row],
        out_shape=[jax.ShapeDtypeStruct((m, d), F32), jax.ShapeDtypeStruct((m, d), BF16)],
        compiler_params=_cparams(("parallel",), 40),
        name="ln2",
    )(s1, s2, g1.reshape(1, d), b1.reshape(1, d), g2.reshape(1, d), b2.reshape(1, d))


def kernel(x, w_in, b_gate, w_short_conv, na_rpb, gla_w_decay, gla_b_decay, gla_norm_g, w_branch, w_out,
           ln1_g, ln1_b, w_ffn_in, w_ffn_conv, b_ffn_conv, w_ffn_out, ln2_g, ln2_b):
    bsz, s, d = x.shape
    depth = w_in.shape[0]
    assert (s, d) == (SEQ, D_MODEL) and w_in.shape[2] == OFF_GATE + N_BRANCH * D_MODEL
    assert w_ffn_in.shape[2] == 2 * D_FF
    alpha = (2.0 * depth) ** 0.25
    ff_pad = D_FF_PAD - D_FF

    w_in_t = jnp.swapaxes(w_in, 1, 2)
    w_br_b = w_branch.astype(BF16)
    w_out_b = w_out.astype(BF16)
    w_conv_ff = jnp.pad(w_ffn_conv, ((0, 0), (0, 0), (0, ff_pad)))
    b_conv_ff = jnp.pad(b_ffn_conv, ((0, 0), (0, ff_pad)))

    xf = x.reshape(bsz * s, d)
    xb = xf.astype(BF16)
    for l in range(depth):
        hm = matmul_wres(xb, w_in_t, l, 0, MAIN_WIDTH, BF16, cast=ffn_in_padcast(w_ffn_in) if l == 0 else None)
        gates = matmul_wres(xb, w_in_t, l, OFF_GATE, N_BRANCH * D_MODEL, BF16, bias=b_gate[l],
                            cast=ffn_out_padcast(w_ffn_out) if l == 0 else None)
        if l == 0:
            (hm, w_ff_b), (gates, w_ffo_b) = hm, gates
        a_lr = matmul_wres(xb, w_in_t, l, MAIN_WIDTH, LANES, F32, tn=LANES)

        y_conv = short_conv_mixer(hm, w_short_conv[l], bsz)
        y_na = neighbourhood_attention(hm, _na_pair_tables(na_rpb[l]), bsz)
        wdec = jnp.zeros((2, LANES, GLA_KEY_WIDTH), F32)
        wdec = wdec.at[0, :GLA_GATE_RANK].set(gla_w_decay[l, 0])
        wdec = wdec.at[1, GLA_GATE_RANK:2 * GLA_GATE_RANK].set(gla_w_decay[l, 1])
        o_f, o_b = gla_scan(hm, a_lr, wdec.astype(BF16), gla_b_decay[l], bsz)
        y_gla = gla_output(o_f, o_b, hm, gla_norm_g[l])

        mixed = gated_mix((y_conv, y_na, y_gla), w_br_b, l, gates)
        s1 = matmul_residual(mixed, w_out_b, l, xf, alpha)
        xb = layer_norm_bf16(s1, ln1_g[l], ln1_b[l])

        g = ffn_in(xb, w_ff_b, l, w_conv_ff[l], b_conv_ff[l].reshape(1, D_FF_PAD), D_FF)
        s2 = matmul_kacc(g, w_ffo_b, l, D_FF)
        xf, xb = layer_norm_ffn_residual(s1, s2, ln1_g[l], ln1_b[l], ln2_g[l], ln2_b[l], alpha)
    return xf.reshape(bsz, s, d)
```

```python
import functools
from typing import Callable, NamedTuple

import numpy as np
import jax
import jax.numpy as jnp
from jax import lax
from jax.experimental import pallas as pl
from jax.experimental.pallas import tpu as pltpu

F32 = jnp.float32
BF16 = jnp.bfloat16

D_MODEL = 4096
SEQ = 4096
GRID_W = 64
ROWS = SEQ // GRID_W
BRANCH_WIDTH = D_MODEL // 2
N_BRANCH = 3
NA_HEAD_DIM = 128
NA_HEADS = BRANCH_WIDTH // NA_HEAD_DIM
NA_WIN_ROWS = 8
NA_WIN_COLS = 16
NA_STEP_ROWS = 8
NA_QROWS = 4
NA_KROWS = 12
NA_HEADS_PER_STEP = 4
GLA_HEADS = 4
GLA_KEY_WIDTH = BRANCH_WIDTH // 2
GLA_KEY_DIM = GLA_KEY_WIDTH // GLA_HEADS
GLA_VALUE_DIM = BRANCH_WIDTH // GLA_HEADS
GLA_GATE_RANK = 16
GLA_GATE_TAU = 16.0
GLA_CHUNK = 64
N_CHUNKS = SEQ // GLA_CHUNK
D_FF = 256 * ((8 * D_MODEL // 3 + 255) // 256)
LN_EPS = 1e-5
RMS_EPS = 1e-6
NEG_INF = -1e30

OFF_CONV_B = 0
OFF_CONV_C = BRANCH_WIDTH
OFF_CONV_H = 2 * BRANCH_WIDTH
OFF_NA_Q = 3 * BRANCH_WIDTH
OFF_NA_K = 4 * BRANCH_WIDTH
OFF_NA_V = 5 * BRANCH_WIDTH
OFF_GLA_Q = 6 * BRANCH_WIDTH
OFF_GLA_K = OFF_GLA_Q + GLA_KEY_WIDTH
OFF_GLA_V = OFF_GLA_K + GLA_KEY_WIDTH
OFF_GLA_R = OFF_GLA_V + BRANCH_WIDTH
MAIN_WIDTH = OFF_GLA_R + BRANCH_WIDTH
OFF_GATE = MAIN_WIDTH + 2 * GLA_GATE_RANK

LANES = 128
V7X_VMEM_BYTES = 64 * 1024 * 1024
MM_TILE = 1024
D_FF_PAD = MM_TILE * (-(-D_FF // MM_TILE))
FFN_HALO = 16
MIB = 1024 * 1024


def _cparams(semantics, vmem_mib):
    assert vmem_mib * MIB < V7X_VMEM_BYTES
    return pltpu.CompilerParams(dimension_semantics=semantics, vmem_limit_bytes=vmem_mib * MIB)


_NT = (((1,), (1,)), ((), ()))


def _mm_residual_kernel(a_ref, b_ref, res_ref, o_ref, *, alpha):
    acc = jnp.dot(a_ref[...], b_ref[...], preferred_element_type=F32)
    o_ref[...] = alpha * res_ref[...] + acc


def matmul_residual(a, w, layer, residual, alpha, tm=MM_TILE, tn=512):
    m, k = a.shape
    n = w.shape[2]
    assert n % tn == 0 and m % tm == 0
    return pl.pallas_call(
        functools.partial(_mm_residual_kernel, alpha=alpha),
        grid=(m // tm, n // tn),
        in_specs=[pl.BlockSpec((tm, k), lambda i, j: (i, 0)),
                  pl.BlockSpec((None, k, tn), lambda i, j: (layer, 0, j)),
                  pl.BlockSpec((tm, tn), lambda i, j: (i, j))],
        out_specs=pl.BlockSpec((tm, tn), lambda i, j: (i, j)),
        out_shape=jax.ShapeDtypeStruct((m, n), F32),
        compiler_params=_cparams(("parallel", "parallel"), 40),
        name="mm_residual",
    )(a, w, residual)


class PadCast(NamedTuple):
    src: jax.Array
    out_shape: tuple
    block: tuple
    n_blocks: int
    plan: Callable


def ffn_in_padcast(w_ffn_in, cb=LANES):
    depth, d, _ = w_ffn_in.shape
    live, padded = D_FF // cb, D_FF_PAD // cb

    def plan(t):
        layer, c = t // (2 * padded), t % (2 * padded)
        half, jj = c // padded, c % padded
        return (layer, 0, half * live + jnp.minimum(jj, live - 1)), (layer, 0, c)

    return PadCast(w_ffn_in, (depth, d, 2 * D_FF_PAD), (None, d, cb), depth * 2 * padded, plan)


def ffn_out_padcast(w_ffn_out, rb=LANES):
    depth, _, d = w_ffn_out.shape
    live, padded = D_FF // rb, D_FF_PAD // rb

    def plan(t):
        layer, r = t // padded, t % padded
        return (layer, jnp.minimum(r, live - 1), 0), (layer, r, 0)

    return PadCast(w_ffn_out, (depth, D_FF_PAD, d), (None, rb, d), depth * padded, plan)


def _mm_wres_kernel(*refs, sigmoid, carries_cast):
    a_ref, w_ref = refs[:2]
    refs = refs[2:]
    acc = lax.dot_general(a_ref[...], w_ref[...].astype(BF16), _NT, preferred_element_type=F32)
    if sigmoid:
        acc = jax.nn.sigmoid(acc + refs[0][...])
        refs = refs[1:]
    if carries_cast:
        src_ref, o_ref, dst_ref = refs
        dst_ref[...] = src_ref[...].astype(dst_ref.dtype)
    else:
        (o_ref,) = refs
    o_ref[...] = acc.astype(o_ref.dtype)


def matmul_wres(a, w_t, layer, row0, n, out_dtype, bias=None, cast=None, tm=MM_TILE, tn=768):
    m, k = a.shape
    assert n % tn == 0 and m % tm == 0 and row0 % 8 == 0
    ni = m // tm
    in_specs = [pl.BlockSpec((tm, k), lambda j, i: (i, 0)),
                pl.BlockSpec((None, pl.Element(tn), pl.Element(k)),
                             lambda j, i: (layer, pl.multiple_of(row0 + j * tn, 8), 0))]
    args = [a, w_t]
    out_specs = [pl.BlockSpec((tm, tn), lambda j, i: (i, j))]
    out_shape = [jax.ShapeDtypeStruct((m, n), out_dtype)]
    if bias is not None:
        in_specs.append(pl.BlockSpec((1, tn), lambda j, i: (0, j)))
        args.append(bias.reshape(1, n))
    if cast is not None:
        assert cast.n_blocks <= (n // tn) * ni

        def step(j, i):
            return jnp.minimum(j * ni + i, cast.n_blocks - 1)

        in_specs.append(pl.BlockSpec(cast.block, lambda j, i: cast.plan(step(j, i))[0]))
        args.append(cast.src)
        out_specs.append(pl.BlockSpec(cast.block, lambda j, i: cast.plan(step(j, i))[1]))
        out_shape.append(jax.ShapeDtypeStruct(cast.out_shape, BF16))
    res = pl.pallas_call(
        functools.partial(_mm_wres_kernel, sigmoid=bias is not None, carries_cast=cast is not None),
        grid=(n // tn, ni),
        in_specs=in_specs,
        out_specs=out_specs,
        out_shape=out_shape,
        compiler_params=(_cparams(("parallel", "parallel"), 52) if cast is None
                         else _cparams(("arbitrary", "arbitrary"), 60)),
        name="mm_wres" if bias is None else "mm_wres_sigmoid",
    )(*args)
    return res if cast is not None else res[0]


def _mm_kacc_kernel(a_ref, b_ref, o_ref, *, k_last):
    kk = pl.program_id(1)
    last = pl.num_programs(1) - 1

    @pl.when(kk == 0)
    def _():
        o_ref[...] = jnp.zeros_like(o_ref)

    @pl.when(kk < last)
    def _():
        o_ref[...] += jnp.dot(a_ref[...], b_ref[...], preferred_element_type=F32)

    @pl.when(kk == last)
    def _():
        o_ref[...] += jnp.dot(a_ref[:, :k_last], b_ref[:k_last, :], preferred_element_type=F32)


def matmul_kacc(a, w, layer, k_live, tm=MM_TILE, tk=MM_TILE):
    m, k = a.shape
    n = w.shape[2]
    assert k % tk == 0 and 0 < k - k_live < tk
    return pl.pallas_call(
        functools.partial(_mm_kacc_kernel, k_last=k_live - (k - tk)),
        grid=(m // tm, k // tk),
        in_specs=[pl.BlockSpec((tm, tk), lambda i, kk: (i, kk)),
                  pl.BlockSpec((None, tk, n), lambda i, kk: (layer, kk, 0))],
        out_specs=pl.BlockSpec((tm, n), lambda i, kk: (i, 0)),
        out_shape=jax.ShapeDtypeStruct((m, n), F32),
        compiler_params=_cparams(("parallel", "arbitrary"), 56),
        name="mm_kacc",
    )(a, w)


def _conv3(p, w_ref):
    t = lax.broadcasted_iota(jnp.int32, p.shape, 0)
    prev = jnp.where(t == 0, 0.0, pltpu.roll(p, 1, 0))
    nxt = jnp.where(t == SEQ - 1, 0.0, pltpu.roll(p, SEQ - 1, 0))
    return prev * w_ref[0:1, :] + p * w_ref[1:2, :] + nxt * w_ref[2:3, :]


def _short_conv_kernel(b_ref, c_ref, h_ref, w_ref, o_ref):
    p = c_ref[...].astype(F32) * h_ref[...].astype(F32)
    o_ref[...] = (b_ref[...].astype(F32) * _conv3(p, w_ref)).astype(o_ref.dtype)


def short_conv_mixer(hm, w_conv, bsz, tc=256):
    nb = BRANCH_WIDTH // tc

    def spec(off):
        return pl.BlockSpec((SEQ, tc), lambda b, j: (b, off // tc + j))

    return pl.pallas_call(
        _short_conv_kernel,
        grid=(bsz, nb),
        in_specs=[spec(OFF_CONV_B), spec(OFF_CONV_C), spec(OFF_CONV_H),
                  pl.BlockSpec((3, tc), lambda b, j: (0, j))],
        out_specs=pl.BlockSpec((SEQ, tc), lambda b, j: (b, j)),
        out_shape=jax.ShapeDtypeStruct((bsz * SEQ, BRANCH_WIDTH), BF16),
        compiler_params=_cparams(("parallel", "parallel"), 48),
        name="short_conv",
    )(hm, hm, hm, w_conv)


def _ffn_in_kernel(xp_ref, x_ref, xn_ref, wa_ref, wu_ref, wc_ref, bc_ref, o_ref, xext_ref, *,
                   tiles_per_seq, n_last):
    i = pl.program_id(0)
    tm = x_ref.shape[0]
    n_ext = tm + 2 * FFN_HALO

    @pl.when(pl.program_id(1) == 0)
    def _():
        xext_ref[0:FFN_HALO] = xp_ref[...]
        xext_ref[FFN_HALO:FFN_HALO + tm] = x_ref[...]
        xext_ref[FFN_HALO + tm:n_ext] = xn_ref[...]

    def compute(ncols):
        cs = slice(0, ncols)
        a_ext = jnp.dot(xext_ref[...], wa_ref[:, cs], preferred_element_type=F32)
        u = jnp.dot(x_ref[...], wu_ref[:, cs], preferred_element_type=F32)
        a = a_ext[FFN_HALO:FFN_HALO + tm]
        prev = pltpu.roll(a_ext, 1, 0)[FFN_HALO:FFN_HALO + tm]
        nxt = pltpu.roll(a_ext, n_ext - 1, 0)[FFN_HALO:FFN_HALO + tm]
        pos = lax.rem(i, tiles_per_seq)
        row = lax.broadcasted_iota(jnp.int32, a.shape, 0)
        prev = jnp.where(row == jnp.where(pos == 0, 0, -1), 0.0, prev)
        nxt = jnp.where(row == jnp.where(pos == tiles_per_seq - 1, tm - 1, -1), 0.0, nxt)
        c = prev * wc_ref[0:1, cs] + a * wc_ref[1:2, cs] + nxt * wc_ref[2:3, cs] + bc_ref[:, cs]
        g = 0.5 * c * (1.0 + lax.erf(c * (2.0 ** -0.5)))
        o_ref[:, cs] = (g * u).astype(o_ref.dtype)
        if ncols < o_ref.shape[1]:
            o_ref[:, ncols:] = jnp.zeros((tm, o_ref.shape[1] - ncols), o_ref.dtype)

    j = pl.program_id(1)
    last = pl.num_programs(1) - 1
    pl.when(j < last)(lambda: compute(o_ref.shape[1]))
    pl.when(j == last)(lambda: compute(n_last))


def ffn_in(xb, w_ff, layer, w_conv, b_conv, n_live, tm=MM_TILE, tn=512):
    m, k = xb.shape
    nf = w_ff.shape[2] // 2
    nb = nf // tn
    hb = tm // FFN_HALO
    assert nf % tn == 0 and 0 < nf - n_live < tn
    return pl.pallas_call(
        functools.partial(_ffn_in_kernel, tiles_per_seq=SEQ // tm, n_last=n_live - (nf - tn)),
        grid=(m // tm, nb),
        in_specs=[pl.BlockSpec((FFN_HALO, k), lambda i, j: (jnp.maximum(i * hb - 1, 0), 0)),
                  pl.BlockSpec((tm, k), lambda i, j: (i, 0)),
                  pl.BlockSpec((FFN_HALO, k), lambda i, j: (jnp.minimum((i + 1) * hb, m // FFN_HALO - 1), 0)),
                  pl.BlockSpec((None, k, tn), lambda i, j: (layer, 0, j)),
                  pl.BlockSpec((None, k, tn), lambda i, j: (layer, 0, nb + j)),
                  pl.BlockSpec((3, tn), lambda i, j: (0, j)),
                  pl.BlockSpec((1, tn), lambda i, j: (0, j))],
        out_specs=pl.BlockSpec((tm, tn), lambda i, j: (i, j)),
        out_shape=jax.ShapeDtypeStruct((m, nf), BF16),
        scratch_shapes=[pltpu.VMEM((tm + 2 * FFN_HALO, k), BF16)],
        compiler_params=_cparams(("parallel", "arbitrary"), 52),
        name="ffn_in",
    )(xb, xb, xb, w_ff, w_ff, w_conv, b_conv)


def _na_key_start(first_query_row):
    return jnp.clip(first_query_row - NA_WIN_ROWS // 2, 0, ROWS - NA_KROWS)


NA_SCALE = NA_HEAD_DIM ** -0.5
LOG2_E = 1.4426950408889634
NA_ROW_OFFSETS = 2 * NA_WIN_ROWS - 1
NA_PAIR_CODES = 3 * NA_ROW_OFFSETS


def _na_pair_tables(rpb):
    qc = np.arange(GRID_W)[:, None]
    kc = np.arange(GRID_W)[None, :]
    wc = np.clip(qc - NA_WIN_COLS // 2, 0, GRID_W - NA_WIN_COLS)
    col_valid = (kc >= wc) & (kc < wc + NA_WIN_COLS)
    cidx = np.clip(kc - qc + NA_WIN_COLS - 1, 0, 2 * NA_WIN_COLS - 2)
    tm = jnp.where(col_valid[None, None], rpb[:, :, cidx] * (1.0 / NA_SCALE), NEG_INF)
    tm = jnp.concatenate([tm, jnp.full_like(tm[:, :1], NEG_INF)], axis=1)
    out = NA_ROW_OFFSETS
    both = np.arange(out - 1)
    one = np.arange(out)
    left = np.concatenate([both, one, np.full(out, out), [out]])
    right = np.concatenate([both + 1, np.full(out, out), one, [out]])
    return jnp.concatenate([tm[:, left], tm[:, right]], axis=-1)


def _na_pair_code(r, kl):
    wr = jnp.clip(r - NA_WIN_ROWS // 2, 0, ROWS - NA_WIN_ROWS)
    dl = kl - r + NA_WIN_ROWS - 1
    vl = jnp.logical_and(kl >= wr, kl < wr + NA_WIN_ROWS)
    vr = jnp.logical_and(kl + 1 >= wr, kl + 1 < wr + NA_WIN_ROWS)
    out = NA_ROW_OFFSETS
    return jnp.where(jnp.logical_and(vl, vr), dl,
                     jnp.where(vl, out - 1 + dl, jnp.where(vr, 2 * out - 1 + dl + 1, 3 * out - 1)))


def _na_kernel(q_ref, k_ref, v_ref, bias_ref, o_ref):
    band = pl.program_id(2)
    nq, nk = NA_QROWS * GRID_W, NA_KROWS * GRID_W
    for grp in range(q_ref.shape[0] // nq):
        r0 = band * (q_ref.shape[0] // GRID_W) + grp * NA_QROWS
        key0 = _na_key_start(r0)
        k0 = pl.multiple_of(key0 * GRID_W, GRID_W)
        qs = slice(grp * nq, (grp + 1) * nq)
        codes = [[_na_pair_code(r0 + i, key0 + 2 * jp) for jp in range(NA_KROWS // 2)] for i in range(NA_QROWS)]
        for h in range(NA_HEADS_PER_STEP):
            cs = slice(h * NA_HEAD_DIM, (h + 1) * NA_HEAD_DIM)
            k = k_ref[pl.ds(k0, nk), cs]
            v = v_ref[pl.ds(k0, nk), cs]
            bias = jnp.concatenate([jnp.concatenate([bias_ref[h, c] for c in row], axis=1) for row in codes],
                                   axis=0)
            t = lax.dot_general(q_ref[qs, cs], k, (((1,), (1,)), ((), ())), preferred_element_type=F32) + bias
            m = jnp.max(t, axis=-1, keepdims=True)
            p = jnp.exp2((t - m) * (NA_SCALE * LOG2_E))
            l = jnp.sum(p, axis=-1, keepdims=True)
            o = jnp.dot(p.astype(BF16), v, preferred_element_type=F32)
            o_ref[qs, cs] = (o / l).astype(o_ref.dtype)


def neighbourhood_attention(hm, bias_tab, bsz):
    nbands = ROWS // NA_STEP_ROWS
    tq = NA_STEP_ROWS * GRID_W
    hp = NA_HEADS_PER_STEP
    wg = hp * NA_HEAD_DIM
    qo, ko, vo = (off // wg for off in (OFF_NA_Q, OFF_NA_K, OFF_NA_V))
    return pl.pallas_call(
        _na_kernel,
        grid=(NA_HEADS // hp, bsz, nbands),
        in_specs=[
            pl.BlockSpec((tq, wg), lambda g, b, r: (b * nbands + r, qo + g)),
            pl.BlockSpec((SEQ, wg), lambda g, b, r: (b, ko + g)),
            pl.BlockSpec((SEQ, wg), lambda g, b, r: (b, vo + g)),
            pl.BlockSpec((hp, NA_PAIR_CODES, GRID_W, 2 * GRID_W), lambda g, b, r: (g, 0, 0, 0)),
        ],
        out_specs=pl.BlockSpec((tq, wg), lambda g, b, r: (b * nbands + r, g)),
        out_shape=jax.ShapeDtypeStruct((bsz * SEQ, BRANCH_WIDTH), BF16),
        compiler_params=_cparams(("parallel", "parallel", "parallel"), 48),
        name="natten",
    )(hm, hm, hm, bias_tab)


def _gla_scan_kernel(qf_ref, kf_ref, vf_ref, af_ref, qb_ref, kb_ref, vb_ref, ab_ref,
                     wdec_ref, bdec_ref, of_ref, ob_ref, state_ref):
    @pl.when(pl.program_id(1) == 0)
    def _():
        state_ref[...] = jnp.zeros_like(state_ref)

    row = lax.broadcasted_iota(jnp.int32, (GLA_CHUNK, GLA_CHUNK), 0)
    col = lax.broadcasted_iota(jnp.int32, (GLA_CHUNK, GLA_CHUNK), 1)
    nt = (((1,), (1,)), ((), ()))
    tn = (((0,), (0,)), ((), ()))
    dirs = ((qf_ref, kf_ref, vf_ref, af_ref, of_ref, row >= col, GLA_CHUNK - 1),
            (qb_ref, kb_ref, vb_ref, ab_ref, ob_ref, row <= col, 0))
    for d, (q_ref, k_ref, v_ref, a_ref, o_ref, tri, last_row) in enumerate(dirs):
        z = jnp.dot(a_ref[...].astype(BF16), wdec_ref[d], preferred_element_type=F32) + bdec_ref[d:d + 1, :]
        log_a = (jnp.minimum(z, 0.0) - jnp.log(1.0 + jnp.exp(-jnp.abs(z)))) * (1.0 / GLA_GATE_TAU)
        cum = jnp.dot(tri.astype(F32), log_a, precision=lax.Precision.HIGHEST, preferred_element_type=F32)
        last = cum[last_row:last_row + 1, :]
        q_in = q_ref[...].astype(F32) * ((GLA_KEY_DIM ** -0.5) * jnp.exp(cum))
        kk = k_ref[...].astype(F32)
        k_in = kk * jnp.exp(-cum)
        k_out = kk * jnp.exp(last - cum)
        dec = jnp.exp(last)
        for h in range(GLA_HEADS):
            ks = slice(h * GLA_KEY_DIM, (h + 1) * GLA_KEY_DIM)
            vs = slice(h * GLA_VALUE_DIM, (h + 1) * GLA_VALUE_DIM)
            qh = q_in[:, ks].astype(BF16)
            vh = v_ref[:, vs]
            att = lax.dot_general(qh, k_in[:, ks].astype(BF16), nt, preferred_element_type=F32)
            att = jnp.where(tri, att, 0.0).astype(BF16)
            st = state_ref[d, h]
            o = jnp.dot(att, vh, preferred_element_type=F32)
            o = o + lax.dot_general(qh, st.astype(BF16), nt, preferred_element_type=F32)
            o_ref[:, vs] = o
            upd = lax.dot_general(vh, k_out[:, ks].astype(BF16), tn, preferred_element_type=F32)
            state_ref[d, h] = st * dec[:, ks] + upd


def gla_scan(hm, a_lr, wdec, bdec, bsz):
    qo = OFF_GLA_Q // GLA_KEY_WIDTH
    ko = OFF_GLA_K // GLA_KEY_WIDTH
    vo = OFF_GLA_V // BRANCH_WIDTH

    def fwd(colblk):
        return lambda b, c: (b * N_CHUNKS + c, colblk)

    def bwd(colblk):
        return lambda b, c: (b * N_CHUNKS + N_CHUNKS - 1 - c, colblk)

    def specs(idx):
        return [pl.BlockSpec((GLA_CHUNK, GLA_KEY_WIDTH), idx(qo)),
                pl.BlockSpec((GLA_CHUNK, GLA_KEY_WIDTH), idx(ko)),
                pl.BlockSpec((GLA_CHUNK, BRANCH_WIDTH), idx(vo)),
                pl.BlockSpec((GLA_CHUNK, LANES), idx(0))]

    out = jax.ShapeDtypeStruct((bsz * SEQ, BRANCH_WIDTH), F32)
    return pl.pallas_call(
        _gla_scan_kernel,
        grid=(bsz, N_CHUNKS),
        in_specs=specs(fwd) + specs(bwd) + [
            pl.BlockSpec((2, LANES, GLA_KEY_WIDTH), lambda b, c: (0, 0, 0)),
            pl.BlockSpec((2, GLA_KEY_WIDTH), lambda b, c: (0, 0))],
        out_specs=[pl.BlockSpec((GLA_CHUNK, BRANCH_WIDTH), fwd(0)),
                   pl.BlockSpec((GLA_CHUNK, BRANCH_WIDTH), bwd(0))],
        out_shape=[out, out],
        scratch_shapes=[pltpu.VMEM((2, GLA_HEADS, GLA_VALUE_DIM, GLA_KEY_DIM), F32)],
        compiler_params=_cparams(("parallel", "arbitrary"), 32),
        name="gla_scan",
    )(hm, hm, hm, a_lr, hm, hm, hm, a_lr, wdec, bdec)


def _gla_out_kernel(of_ref, ob_ref, r_ref, g_ref, o_ref):
    for h in range(GLA_HEADS):
        vs = slice(h * GLA_VALUE_DIM, (h + 1) * GLA_VALUE_DIM)
        o = of_ref[:, vs] + ob_ref[:, vs]
        o = o * lax.rsqrt(jnp.mean(o * o, axis=-1, keepdims=True) + RMS_EPS) * g_ref[:, vs]
        r = r_ref[:, vs].astype(F32)
        o_ref[:, vs] = (o * (r * jax.nn.sigmoid(r))).astype(o_ref.dtype)


def gla_output(o_f, o_b, hm, norm_g, tm=512):
    m = o_f.shape[0]
    row = pl.BlockSpec((tm, BRANCH_WIDTH), lambda i: (i, 0))
    return pl.pallas_call(
        _gla_out_kernel,
        grid=(m // tm,),
        in_specs=[row, row,
                  pl.BlockSpec((tm, BRANCH_WIDTH), lambda i: (i, OFF_GLA_R // BRANCH_WIDTH)),
                  pl.BlockSpec((1, BRANCH_WIDTH), lambda i: (0, 0))],
        out_specs=row,
        out_shape=jax.ShapeDtypeStruct((m, BRANCH_WIDTH), BF16),
        compiler_params=_cparams(("parallel",), 40),
        name="gla_output",
    )(o_f, o_b, hm, norm_g.reshape(1, BRANCH_WIDTH))


def _mix_kernel(y0_ref, y1_ref, y2_ref, w_ref, g0_ref, g1_ref, g2_ref, o_ref):
    acc = None
    for i, (y_ref, g_ref) in enumerate(((y0_ref, g0_ref), (y1_ref, g1_ref), (y2_ref, g2_ref))):
        t = g_ref[...].astype(F32) * jnp.dot(y_ref[...], w_ref[i], preferred_element_type=F32)
        acc = t if acc is None else acc + t
    o_ref[...] = acc.astype(o_ref.dtype)


def gated_mix(ys, w_branch, layer, gates, tm=MM_TILE, tn=512):
    m = ys[0].shape[0]
    nj = D_MODEL // tn
    yspec = pl.BlockSpec((tm, BRANCH_WIDTH), lambda i, j: (i, 0))

    def gspec(br):
        return pl.BlockSpec((tm, tn), lambda i, j: (i, br * nj + j))

    return pl.pallas_call(
        _mix_kernel,
        grid=(m // tm, nj),
        in_specs=[yspec, yspec, yspec,
                  pl.BlockSpec((None, N_BRANCH, BRANCH_WIDTH, tn), lambda i, j: (layer, 0, 0, j)),
                  gspec(0), gspec(1), gspec(2)],
        out_specs=pl.BlockSpec((tm, tn), lambda i, j: (i, j)),
        out_shape=jax.ShapeDtypeStruct((m, D_MODEL), BF16),
        compiler_params=_cparams(("parallel", "parallel"), 52),
        name="gated_mix",
    )(*ys, w_branch, gates, gates, gates)


def _layer_norm(y, g_ref, b_ref):
    mu = jnp.mean(y, axis=-1, keepdims=True)
    yc = y - mu
    var = jnp.mean(yc * yc, axis=-1, keepdims=True)
    return yc * lax.rsqrt(var + LN_EPS) * g_ref[...] + b_ref[...]


def _ln1_kernel(s_ref, g_ref, b_ref, ob_ref):
    ob_ref[...] = _layer_norm(s_ref[...], g_ref, b_ref).astype(BF16)


def _ln2_kernel(s1_ref, s2_ref, g1_ref, b1_ref, g2_ref, b2_ref, of_ref, ob_ref, *, alpha):
    x1 = _layer_norm(s1_ref[...], g1_ref, b1_ref)
    out = _layer_norm(alpha * x1 + s2_ref[...], g2_ref, b2_ref)
    of_ref[...] = out
    ob_ref[...] = out.astype(BF16)


LN_ROWS = 256


def layer_norm_bf16(s1, g1, b1):
    m, d = s1.shape
    row = pl.BlockSpec((LN_ROWS, d), lambda i: (i, 0))
    vec = pl.BlockSpec((1, d), lambda i: (0, 0))
    return pl.pallas_call(
        _ln1_kernel,
        grid=(m // LN_ROWS,),
        in_specs=[row, vec, vec],
        out_specs=row,
        out_shape=jax.ShapeDtypeStruct((m, d), BF16),
        compiler_params=_cparams(("parallel",), 40),
        name="ln1",
    )(s1, g1.reshape(1, d), b1.reshape(1, d))


def layer_norm_ffn_residual(s1, s2, g1, b1, g2, b2, alpha):
    m, d = s1.shape
    row = pl.BlockSpec((LN_ROWS, d), lambda i: (i, 0))
    vec = pl.BlockSpec((1, d), lambda i: (0, 0))
    return pl.pallas_call(
        functools.partial(_ln2_kernel, alpha=alpha),
        grid=(m // LN_ROWS,),
        in_specs=[row, row, vec, vec, vec, vec],
        out_specs=[row, row],
        out_shape=[jax.ShapeDtypeStruct((m, d), F32), jax.ShapeDtypeStruct((m, d), BF16)],
        compiler_params=_cparams(("parallel",), 40),
        name="ln2",
    )(s1, s2, g1.reshape(1, d), b1.reshape(1, d), g2.reshape(1, d), b2.reshape(1, d))


def kernel(x, w_in, b_gate, w_short_conv, na_rpb, gla_w_decay, gla_b_decay, gla_norm_g, w_branch, w_out,
           ln1_g, ln1_b, w_ffn_in, w_ffn_conv, b_ffn_conv, w_ffn_out, ln2_g, ln2_b):
    bsz, s, d = x.shape
    depth = w_in.shape[0]
    assert (s, d) == (SEQ, D_MODEL) and w_in.shape[2] == OFF_GATE + N_BRANCH * D_MODEL
    assert w_ffn_in.shape[2] == 2 * D_FF
    alpha = (2.0 * depth) ** 0.25
    ff_pad = D_FF_PAD - D_FF

    w_in_t = jnp.swapaxes(w_in, 1, 2)
    w_br_b = w_branch.astype(BF16)
    w_out_b = w_out.astype(BF16)
    w_conv_ff = jnp.pad(w_ffn_conv, ((0, 0), (0, 0), (0, ff_pad)))
    b_conv_ff = jnp.pad(b_ffn_conv, ((0, 0), (0, ff_pad)))

    xf = x.reshape(bsz * s, d)
    xb = xf.astype(BF16)
    for l in range(depth):
        hm = matmul_wres(xb, w_in_t, l, 0, MAIN_WIDTH, BF16, cast=ffn_in_padcast(w_ffn_in) if l == 0 else None)
        gates = matmul_wres(xb, w_in_t, l, OFF_GATE, N_BRANCH * D_MODEL, BF16, bias=b_gate[l],
                            cast=ffn_out_padcast(w_ffn_out) if l == 0 else None)
        if l == 0:
            (hm, w_ff_b), (gates, w_ffo_b) = hm, gates
        a_lr = matmul_wres(xb, w_in_t, l, MAIN_WIDTH, LANES, F32, tn=LANES)

        y_conv = short_conv_mixer(hm, w_short_conv[l], bsz)
        y_na = neighbourhood_attention(hm, _na_pair_tables(na_rpb[l]), bsz)
        wdec = jnp.zeros((2, LANES, GLA_KEY_WIDTH), F32)
        wdec = wdec.at[0, :GLA_GATE_RANK].set(gla_w_decay[l, 0])
        wdec = wdec.at[1, GLA_GATE_RANK:2 * GLA_GATE_RANK].set(gla_w_decay[l, 1])
        o_f, o_b = gla_scan(hm, a_lr, wdec.astype(BF16), gla_b_decay[l], bsz)
        y_gla = gla_output(o_f, o_b, hm, gla_norm_g[l])

        mixed = gated_mix((y_conv, y_na, y_gla), w_br_b, l, gates)
        s1 = matmul_residual(mixed, w_out_b, l, xf, alpha)
        xb = layer_norm_bf16(s1, ln1_g[l], ln1_b[l])

        g = ffn_in(xb, w_ff_b, l, w_conv_ff[l], b_conv_ff[l].reshape(1, D_FF_PAD), D_FF)
        s2 = matmul_kacc(g, w_ffo_b, l, D_FF)
        xf, xb = layer_norm_ffn_residual(s1, s2, ln1_g[l], ln1_b[l], ln2_g[l], ln2_b[l], alpha)
    return xf.reshape(bsz, s, d)
```

```python
import functools
from typing import Callable, NamedTuple

import numpy as np
import jax
import jax.numpy as jnp
from jax import lax
from jax.experimental import pallas as pl
from jax.experimental.pallas import tpu as pltpu

F32 = jnp.float32
BF16 = jnp.bfloat16

D_MODEL = 4096
SEQ = 4096
GRID_W = 64
ROWS = SEQ // GRID_W
BRANCH_WIDTH = D_MODEL // 2
N_BRANCH = 3
NA_HEAD_DIM = 128
NA_HEADS = BRANCH_WIDTH // NA_HEAD_DIM
NA_WIN_ROWS = 8
NA_WIN_COLS = 16
NA_QROWS = 8
NA_KROWS = 16
NA_HEADS_PER_STEP = 4
GLA_HEADS = 4
GLA_KEY_WIDTH = BRANCH_WIDTH // 2
GLA_KEY_DIM = GLA_KEY_WIDTH // GLA_HEADS
GLA_VALUE_DIM = BRANCH_WIDTH // GLA_HEADS
GLA_GATE_RANK = 16
GLA_GATE_TAU = 16.0
GLA_CHUNK = 64
N_CHUNKS = SEQ // GLA_CHUNK
D_FF = 256 * ((8 * D_MODEL // 3 + 255) // 256)
LN_EPS = 1e-5
RMS_EPS = 1e-6
NEG_INF = -1e30

OFF_CONV_B = 0
OFF_CONV_C = BRANCH_WIDTH
OFF_CONV_H = 2 * BRANCH_WIDTH
OFF_NA_Q = 3 * BRANCH_WIDTH
OFF_NA_K = 4 * BRANCH_WIDTH
OFF_NA_V = 5 * BRANCH_WIDTH
OFF_GLA_Q = 6 * BRANCH_WIDTH
OFF_GLA_K = OFF_GLA_Q + GLA_KEY_WIDTH
OFF_GLA_V = OFF_GLA_K + GLA_KEY_WIDTH
OFF_GLA_R = OFF_GLA_V + BRANCH_WIDTH
MAIN_WIDTH = OFF_GLA_R + BRANCH_WIDTH
OFF_GATE = MAIN_WIDTH + 2 * GLA_GATE_RANK

LANES = 128
V7X_VMEM_BYTES = 64 * 1024 * 1024
MM_TILE = 1024
D_FF_PAD = MM_TILE * (-(-D_FF // MM_TILE))
FFN_HALO = 16
MIB = 1024 * 1024


def _cparams(semantics, vmem_mib):
    assert vmem_mib * MIB < V7X_VMEM_BYTES
    return pltpu.CompilerParams(dimension_semantics=semantics, vmem_limit_bytes=vmem_mib * MIB)


_NT = (((1,), (1,)), ((), ()))


def _mm_residual_kernel(a_ref, b_ref, res_ref, o_ref, *, alpha):
    acc = jnp.dot(a_ref[...], b_ref[...], preferred_element_type=F32)
    o_ref[...] = alpha * res_ref[...] + acc


def matmul_residual(a, w, layer, residual, alpha, tm=MM_TILE, tn=512):
    m, k = a.shape
    n = w.shape[2]
    assert n % tn == 0 and m % tm == 0
    return pl.pallas_call(
        functools.partial(_mm_residual_kernel, alpha=alpha),
        grid=(m // tm, n // tn),
        in_specs=[pl.BlockSpec((tm, k), lambda i, j: (i, 0)),
                  pl.BlockSpec((None, k, tn), lambda i, j: (layer, 0, j)),
                  pl.BlockSpec((tm, tn), lambda i, j: (i, j))],
        out_specs=pl.BlockSpec((tm, tn), lambda i, j: (i, j)),
        out_shape=jax.ShapeDtypeStruct((m, n), F32),
        compiler_params=_cparams(("parallel", "parallel"), 40),
        name="mm_residual",
    )(a, w, residual)


class PadCast(NamedTuple):
    src: jax.Array
    out_shape: tuple
    block: tuple
    n_blocks: int
    plan: Callable


def ffn_in_padcast(w_ffn_in, cb=LANES):
    depth, d, _ = w_ffn_in.shape
    live, padded = D_FF // cb, D_FF_PAD // cb

    def plan(t):
        layer, c = t // (2 * padded), t % (2 * padded)
        half, jj = c // padded, c % padded
        return (layer, 0, half * live + jnp.minimum(jj, live - 1)), (layer, 0, c)

    return PadCast(w_ffn_in, (depth, d, 2 * D_FF_PAD), (None, d, cb), depth * 2 * padded, plan)


def ffn_out_padcast(w_ffn_out, rb=LANES):
    depth, _, d = w_ffn_out.shape
    live, padded = D_FF // rb, D_FF_PAD // rb

    def plan(t):
        layer, r = t // padded, t % padded
        return (layer, jnp.minimum(r, live - 1), 0), (layer, r, 0)

    return PadCast(w_ffn_out, (depth, D_FF_PAD, d), (None, rb, d), depth * padded, plan)


def _mm_wres_kernel(*refs, sigmoid, carries_cast):
    a_ref, w_ref = refs[:2]
    refs = refs[2:]
    acc = lax.dot_general(a_ref[...], w_ref[...].astype(BF16), _NT, preferred_element_type=F32)
    if sigmoid:
        acc = jax.nn.sigmoid(acc + refs[0][...])
        refs = refs[1:]
    if carries_cast:
        src_ref, o_ref, dst_ref = refs
        dst_ref[...] = src_ref[...].astype(dst_ref.dtype)
    else:
        (o_ref,) = refs
    o_ref[...] = acc.astype(o_ref.dtype)


def matmul_wres(a, w_t, layer, row0, n, out_dtype, bias=None, cast=None, tm=MM_TILE, tn=768):
    m, k = a.shape
    assert n % tn == 0 and m % tm == 0 and row0 % 8 == 0
    ni = m // tm
    in_specs = [pl.BlockSpec((tm, k), lambda j, i: (i, 0)),
                pl.BlockSpec((None, pl.Element(tn), pl.Element(k)),
                             lambda j, i: (layer, pl.multiple_of(row0 + j * tn, 8), 0))]
    args = [a, w_t]
    out_specs = [pl.BlockSpec((tm, tn), lambda j, i: (i, j))]
    out_shape = [jax.ShapeDtypeStruct((m, n), out_dtype)]
    if bias is not None:
        in_specs.append(pl.BlockSpec((1, tn), lambda j, i: (0, j)))
        args.append(bias.reshape(1, n))
    if cast is not None:
        assert cast.n_blocks <= (n // tn) * ni

        def step(j, i):
            return jnp.minimum(j * ni + i, cast.n_blocks - 1)

        in_specs.append(pl.BlockSpec(cast.block, lambda j, i: cast.plan(step(j, i))[0]))
        args.append(cast.src)
        out_specs.append(pl.BlockSpec(cast.block, lambda j, i: cast.plan(step(j, i))[1]))
        out_shape.append(jax.ShapeDtypeStruct(cast.out_shape, BF16))
    res = pl.pallas_call(
        functools.partial(_mm_wres_kernel, sigmoid=bias is not None, carries_cast=cast is not None),
        grid=(n // tn, ni),
        in_specs=in_specs,
        out_specs=out_specs,
        out_shape=out_shape,
        compiler_params=(_cparams(("parallel", "parallel"), 52) if cast is None
                         else _cparams(("arbitrary", "arbitrary"), 60)),
        name="mm_wres" if bias is None else "mm_wres_sigmoid",
    )(*args)
    return res if cast is not None else res[0]


def _mm_kacc_kernel(a_ref, b_ref, o_ref, *, k_last):
    kk = pl.program_id(1)
    last = pl.num_programs(1) - 1

    @pl.when(kk == 0)
    def _():
        o_ref[...] = jnp.zeros_like(o_ref)

    @pl.when(kk < last)
    def _():
        o_ref[...] += jnp.dot(a_ref[...], b_ref[...], preferred_element_type=F32)

    @pl.when(kk == last)
    def _():
        o_ref[...] += jnp.dot(a_ref[:, :k_last], b_ref[:k_last, :], preferred_element_type=F32)


def matmul_kacc(a, w, layer, k_live, tm=MM_TILE, tk=MM_TILE):
    m, k = a.shape
    n = w.shape[2]
    assert k % tk == 0 and 0 < k - k_live < tk
    return pl.pallas_call(
        functools.partial(_mm_kacc_kernel, k_last=k_live - (k - tk)),
        grid=(m // tm, k // tk),
        in_specs=[pl.BlockSpec((tm, tk), lambda i, kk: (i, kk)),
                  pl.BlockSpec((None, tk, n), lambda i, kk: (layer, kk, 0))],
        out_specs=pl.BlockSpec((tm, n), lambda i, kk: (i, 0)),
        out_shape=jax.ShapeDtypeStruct((m, n), F32),
        compiler_params=_cparams(("parallel", "arbitrary"), 56),
        name="mm_kacc",
    )(a, w)


def _conv3(p, w_ref):
    t = lax.broadcasted_iota(jnp.int32, p.shape, 0)
    prev = jnp.where(t == 0, 0.0, pltpu.roll(p, 1, 0))
    nxt = jnp.where(t == SEQ - 1, 0.0, pltpu.roll(p, SEQ - 1, 0))
    return prev * w_ref[0:1, :] + p * w_ref[1:2, :] + nxt * w_ref[2:3, :]


def _short_conv_kernel(b_ref, c_ref, h_ref, w_ref, o_ref):
    p = c_ref[...].astype(F32) * h_ref[...].astype(F32)
    o_ref[...] = (b_ref[...].astype(F32) * _conv3(p, w_ref)).astype(o_ref.dtype)


def short_conv_mixer(hm, w_conv, bsz, tc=256):
    nb = BRANCH_WIDTH // tc

    def spec(off):
        return pl.BlockSpec((SEQ, tc), lambda b, j: (b, off // tc + j))

    return pl.pallas_call(
        _short_conv_kernel,
        grid=(bsz, nb),
        in_specs=[spec(OFF_CONV_B), spec(OFF_CONV_C), spec(OFF_CONV_H),
                  pl.BlockSpec((3, tc), lambda b, j: (0, j))],
        out_specs=pl.BlockSpec((SEQ, tc), lambda b, j: (b, j)),
        out_shape=jax.ShapeDtypeStruct((bsz * SEQ, BRANCH_WIDTH), BF16),
        compiler_params=_cparams(("parallel", "parallel"), 48),
        name="short_conv",
    )(hm, hm, hm, w_conv)


def _ffn_in_kernel(xp_ref, x_ref, xn_ref, wa_ref, wu_ref, wc_ref, bc_ref, o_ref, xext_ref, *,
                   tiles_per_seq, n_last):
    i = pl.program_id(0)
    tm = x_ref.shape[0]
    n_ext = tm + 2 * FFN_HALO

    @pl.when(pl.program_id(1) == 0)
    def _():
        xext_ref[0:FFN_HALO] = xp_ref[...]
        xext_ref[FFN_HALO:FFN_HALO + tm] = x_ref[...]
        xext_ref[FFN_HALO + tm:n_ext] = xn_ref[...]

    def compute(ncols):
        cs = slice(0, ncols)
        a_ext = jnp.dot(xext_ref[...], wa_ref[:, cs], preferred_element_type=F32)
        u = jnp.dot(x_ref[...], wu_ref[:, cs], preferred_element_type=F32)
        a = a_ext[FFN_HALO:FFN_HALO + tm]
        prev = pltpu.roll(a_ext, 1, 0)[FFN_HALO:FFN_HALO + tm]
        nxt = pltpu.roll(a_ext, n_ext - 1, 0)[FFN_HALO:FFN_HALO + tm]
        pos = lax.rem(i, tiles_per_seq)
        row = lax.broadcasted_iota(jnp.int32, a.shape, 0)
        prev = jnp.where(row == jnp.where(pos == 0, 0, -1), 0.0, prev)
        nxt = jnp.where(row == jnp.where(pos == tiles_per_seq - 1, tm - 1, -1), 0.0, nxt)
        c = prev * wc_ref[0:1, cs] + a * wc_ref[1:2, cs] + nxt * wc_ref[2:3, cs] + bc_ref[:, cs]
        g = 0.5 * c * (1.0 + lax.erf(c * (2.0 ** -0.5)))
        o_ref[:, cs] = (g * u).astype(o_ref.dtype)
        if ncols < o_ref.shape[1]:
            o_ref[:, ncols:] = jnp.zeros((tm, o_ref.shape[1] - ncols), o_ref.dtype)

    j = pl.program_id(1)
    last = pl.num_programs(1) - 1
    pl.when(j < last)(lambda: compute(o_ref.shape[1]))
    pl.when(j == last)(lambda: compute(n_last))


def ffn_in(xb, w_ff, layer, w_conv, b_conv, n_live, tm=MM_TILE, tn=512):
    m, k = xb.shape
    nf = w_ff.shape[2] // 2
    nb = nf // tn
    hb = tm // FFN_HALO
    assert nf % tn == 0 and 0 < nf - n_live < tn
    return pl.pallas_call(
        functools.partial(_ffn_in_kernel, tiles_per_seq=SEQ // tm, n_last=n_live - (nf - tn)),
        grid=(m // tm, nb),
        in_specs=[pl.BlockSpec((FFN_HALO, k), lambda i, j: (jnp.maximum(i * hb - 1, 0), 0)),
                  pl.BlockSpec((tm, k), lambda i, j: (i, 0)),
                  pl.BlockSpec((FFN_HALO, k), lambda i, j: (jnp.minimum((i + 1) * hb, m // FFN_HALO - 1), 0)),
                  pl.BlockSpec((None, k, tn), lambda i, j: (layer, 0, j)),
                  pl.BlockSpec((None, k, tn), lambda i, j: (layer, 0, nb + j)),
                  pl.BlockSpec((3, tn), lambda i, j: (0, j)),
                  pl.BlockSpec((1, tn), lambda i, j: (0, j))],
        out_specs=pl.BlockSpec((tm, tn), lambda i, j: (i, j)),
        out_shape=jax.ShapeDtypeStruct((m, nf), BF16),
        scratch_shapes=[pltpu.VMEM((tm + 2 * FFN_HALO, k), BF16)],
        compiler_params=_cparams(("parallel", "arbitrary"), 52),
        name="ffn_in",
    )(xb, xb, xb, w_ff, w_ff, w_conv, b_conv)


def _na_key_start(band):
    lo = band * NA_QROWS - NA_WIN_ROWS // 2
    hi = ROWS - NA_KROWS
    if isinstance(band, int):
        return min(max(lo, 0), hi)
    return jnp.clip(lo, 0, hi)


NA_SCALE = NA_HEAD_DIM ** -0.5
LOG2_E = 1.4426950408889634
NA_ROW_OFFSETS = 2 * NA_WIN_ROWS - 1
NA_PAIR_CODES = 3 * NA_ROW_OFFSETS


def _na_pair_tables(rpb):
    qc = np.arange(GRID_W)[:, None]
    kc = np.arange(GRID_W)[None, :]
    wc = np.clip(qc - NA_WIN_COLS // 2, 0, GRID_W - NA_WIN_COLS)
    col_valid = (kc >= wc) & (kc < wc + NA_WIN_COLS)
    cidx = np.clip(kc - qc + NA_WIN_COLS - 1, 0, 2 * NA_WIN_COLS - 2)
    tm = jnp.where(col_valid[None, None], rpb[:, :, cidx] * (1.0 / NA_SCALE), NEG_INF)
    tm = jnp.concatenate([tm, jnp.full_like(tm[:, :1], NEG_INF)], axis=1)
    out = NA_ROW_OFFSETS
    both = np.arange(out - 1)
    one = np.arange(out)
    left = np.concatenate([both, one, np.full(out, out), [out]])
    right = np.concatenate([both + 1, np.full(out, out), one, [out]])
    return jnp.concatenate([tm[:, left], tm[:, right]], axis=-1)


def _na_pair_code(band, i, jp):
    r = band * NA_QROWS + i
    wr = jnp.clip(r - NA_WIN_ROWS // 2, 0, ROWS - NA_WIN_ROWS)
    kl = _na_key_start(band) + 2 * jp
    dl = kl - r + NA_WIN_ROWS - 1
    vl = jnp.logical_and(kl >= wr, kl < wr + NA_WIN_ROWS)
    vr = jnp.logical_and(kl + 1 >= wr, kl + 1 < wr + NA_WIN_ROWS)
    out = NA_ROW_OFFSETS
    return jnp.where(jnp.logical_and(vl, vr), dl,
                     jnp.where(vl, out - 1 + dl, jnp.where(vr, 2 * out - 1 + dl + 1, 3 * out - 1)))


def _na_kernel(q_ref, k_ref, v_ref, bias_ref, o_ref):
    band = pl.program_id(2)
    k0 = pl.multiple_of(_na_key_start(band) * GRID_W, GRID_W)
    nk = NA_KROWS * GRID_W
    codes = [[_na_pair_code(band, i, jp) for jp in range(NA_KROWS // 2)] for i in range(NA_QROWS)]
    for h in range(NA_HEADS_PER_STEP):
        cs = slice(h * NA_HEAD_DIM, (h + 1) * NA_HEAD_DIM)
        k = k_ref[pl.ds(k0, nk), cs]
        v = v_ref[pl.ds(k0, nk), cs]
        bias = jnp.concatenate([jnp.concatenate([bias_ref[h, c] for c in row], axis=1) for row in codes], axis=0)
        t = lax.dot_general(q_ref[:, cs], k, (((1,), (1,)), ((), ())), preferred_element_type=F32) + bias
        m = jnp.max(t, axis=-1, keepdims=True)
        p = jnp.exp2((t - m) * (NA_SCALE * LOG2_E))
        l = jnp.sum(p, axis=-1, keepdims=True)
        o = jnp.dot(p.astype(BF16), v, preferred_element_type=F32)
        o_ref[:, cs] = (o / l).astype(o_ref.dtype)


def neighbourhood_attention(hm, bias_tab, bsz):
    nbands = ROWS // NA_QROWS
    tq = NA_QROWS * GRID_W
    hp = NA_HEADS_PER_STEP
    wg = hp * NA_HEAD_DIM
    qo, ko, vo = (off // wg for off in (OFF_NA_Q, OFF_NA_K, OFF_NA_V))
    return pl.pallas_call(
        _na_kernel,
        grid=(NA_HEADS // hp, bsz, nbands),
        in_specs=[
            pl.BlockSpec((tq, wg), lambda g, b, r: (b * nbands + r, qo + g)),
            pl.BlockSpec((SEQ, wg), lambda g, b, r: (b, ko + g)),
            pl.BlockSpec((SEQ, wg), lambda g, b, r: (b, vo + g)),
            pl.BlockSpec((hp, NA_PAIR_CODES, GRID_W, 2 * GRID_W), lambda g, b, r: (g, 0, 0, 0)),
        ],
        out_specs=pl.BlockSpec((tq, wg), lambda g, b, r: (b * nbands + r, g)),
        out_shape=jax.ShapeDtypeStruct((bsz * SEQ, BRANCH_WIDTH), BF16),
        compiler_params=_cparams(("parallel", "parallel", "parallel"), 48),
        name="natten",
    )(hm, hm, hm, bias_tab)


def _gla_scan_kernel(qf_ref, kf_ref, vf_ref, af_ref, qb_ref, kb_ref, vb_ref, ab_ref,
                     wdec_ref, bdec_ref, of_ref, ob_ref, state_ref):
    @pl.when(pl.program_id(1) == 0)
    def _():
        state_ref[...] = jnp.zeros_like(state_ref)

    row = lax.broadcasted_iota(jnp.int32, (GLA_CHUNK, GLA_CHUNK), 0)
    col = lax.broadcasted_iota(jnp.int32, (GLA_CHUNK, GLA_CHUNK), 1)
    nt = (((1,), (1,)), ((), ()))
    tn = (((0,), (0,)), ((), ()))
    dirs = ((qf_ref, kf_ref, vf_ref, af_ref, of_ref, row >= col, GLA_CHUNK - 1),
            (qb_ref, kb_ref, vb_ref, ab_ref, ob_ref, row <= col, 0))
    for d, (q_ref, k_ref, v_ref, a_ref, o_ref, tri, last_row) in enumerate(dirs):
        z = jnp.dot(a_ref[...].astype(BF16), wdec_ref[d], preferred_element_type=F32) + bdec_ref[d:d + 1, :]
        log_a = (jnp.minimum(z, 0.0) - jnp.log(1.0 + jnp.exp(-jnp.abs(z)))) * (1.0 / GLA_GATE_TAU)
        cum = jnp.dot(tri.astype(F32), log_a, precision=lax.Precision.HIGHEST, preferred_element_type=F32)
        last = cum[last_row:last_row + 1, :]
        q_in = q_ref[...].astype(F32) * ((GLA_KEY_DIM ** -0.5) * jnp.exp(cum))
        kk = k_ref[...].astype(F32)
        k_in = kk * jnp.exp(-cum)
        k_out = kk * jnp.exp(last - cum)
        dec = jnp.exp(last)
        for h in range(GLA_HEADS):
            ks = slice(h * GLA_KEY_DIM, (h + 1) * GLA_KEY_DIM)
            vs = slice(h * GLA_VALUE_DIM, (h + 1) * GLA_VALUE_DIM)
            qh = q_in[:, ks].astype(BF16)
            vh = v_ref[:, vs]
            att = lax.dot_general(qh, k_in[:, ks].astype(BF16), nt, preferred_element_type=F32)
            att = jnp.where(tri, att, 0.0).astype(BF16)
            st = state_ref[d, h]
            o = jnp.dot(att, vh, preferred_element_type=F32)
            o = o + lax.dot_general(qh, st.astype(BF16), nt, preferred_element_type=F32)
            o_ref[:, vs] = o
            upd = lax.dot_general(vh, k_out[:, ks].astype(BF16), tn, preferred_element_type=F32)
            state_ref[d, h] = st * dec[:, ks] + upd


def gla_scan(hm, a_lr, wdec, bdec, bsz):
    qo = OFF_GLA_Q // GLA_KEY_WIDTH
    ko = OFF_GLA_K // GLA_KEY_WIDTH
    vo = OFF_GLA_V // BRANCH_WIDTH

    def fwd(colblk):
        return lambda b, c: (b * N_CHUNKS + c, colblk)

    def bwd(colblk):
        return lambda b, c: (b * N_CHUNKS + N_CHUNKS - 1 - c, colblk)

    def specs(idx):
        return [pl.BlockSpec((GLA_CHUNK, GLA_KEY_WIDTH), idx(qo)),
                pl.BlockSpec((GLA_CHUNK, GLA_KEY_WIDTH), idx(ko)),
                pl.BlockSpec((GLA_CHUNK, BRANCH_WIDTH), idx(vo)),
                pl.BlockSpec((GLA_CHUNK, LANES), idx(0))]

    out = jax.ShapeDtypeStruct((bsz * SEQ, BRANCH_WIDTH), F32)
    return pl.pallas_call(
        _gla_scan_kernel,
        grid=(bsz, N_CHUNKS),
        in_specs=specs(fwd) + specs(bwd) + [
            pl.BlockSpec((2, LANES, GLA_KEY_WIDTH), lambda b, c: (0, 0, 0)),
            pl.BlockSpec((2, GLA_KEY_WIDTH), lambda b, c: (0, 0))],
        out_specs=[pl.BlockSpec((GLA_CHUNK, BRANCH_WIDTH), fwd(0)),
                   pl.BlockSpec((GLA_CHUNK, BRANCH_WIDTH), bwd(0))],
        out_shape=[out, out],
        scratch_shapes=[pltpu.VMEM((2, GLA_HEADS, GLA_VALUE_DIM, GLA_KEY_DIM), F32)],
        compiler_params=_cparams(("parallel", "arbitrary"), 32),
        name="gla_scan",
    )(hm, hm, hm, a_lr, hm, hm, hm, a_lr, wdec, bdec)


def _gla_out_kernel(of_ref, ob_ref, r_ref, g_ref, o_ref):
    for h in range(GLA_HEADS):
        vs = slice(h * GLA_VALUE_DIM, (h + 1) * GLA_VALUE_DIM)
        o = of_ref[:, vs] + ob_ref[:, vs]
        o = o * lax.rsqrt(jnp.mean(o * o, axis=-1, keepdims=True) + RMS_EPS) * g_ref[:, vs]
        r = r_ref[:, vs].astype(F32)
        o_ref[:, vs] = (o * (r * jax.nn.sigmoid(r))).astype(o_ref.dtype)


def gla_output(o_f, o_b, hm, norm_g, tm=512):
    m = o_f.shape[0]
    row = pl.BlockSpec((tm, BRANCH_WIDTH), lambda i: (i, 0))
    return pl.pallas_call(
        _gla_out_kernel,
        grid=(m // tm,),
        in_specs=[row, row,
                  pl.BlockSpec((tm, BRANCH_WIDTH), lambda i: (i, OFF_GLA_R // BRANCH_WIDTH)),
                  pl.BlockSpec((1, BRANCH_WIDTH), lambda i: (0, 0))],
        out_specs=row,
        out_shape=jax.ShapeDtypeStruct((m, BRANCH_WIDTH), BF16),
        compiler_params=_cparams(("parallel",), 40),
        name="gla_output",
    )(o_f, o_b, hm, norm_g.reshape(1, BRANCH_WIDTH))


def _mix_kernel(y0_ref, y1_ref, y2_ref, w_ref, g0_ref, g1_ref, g2_ref, o_ref):
    acc = None
    for i, (y_ref, g_ref) in enumerate(((y0_ref, g0_ref), (y1_ref, g1_ref), (y2_ref, g2_ref))):
        t = g_ref[...].astype(F32) * jnp.dot(y_ref[...], w_ref[i], preferred_element_type=F32)
        acc = t if acc is None else acc + t
    o_ref[...] = acc.astype(o_ref.dtype)


def gated_mix(ys, w_branch, layer, gates, tm=MM_TILE, tn=512):
    m = ys[0].shape[0]
    nj = D_MODEL // tn
    yspec = pl.BlockSpec((tm, BRANCH_WIDTH), lambda i, j: (i, 0))

    def gspec(br):
        return pl.BlockSpec((tm, tn), lambda i, j: (i, br * nj + j))

    return pl.pallas_call(
        _mix_kernel,
        grid=(m // tm, nj),
        in_specs=[yspec, yspec, yspec,
                  pl.BlockSpec((None, N_BRANCH, BRANCH_WIDTH, tn), lambda i, j: (layer, 0, 0, j)),
                  gspec(0), gspec(1), gspec(2)],
        out_specs=pl.BlockSpec((tm, tn), lambda i, j: (i, j)),
        out_shape=jax.ShapeDtypeStruct((m, D_MODEL), BF16),
        compiler_params=_cparams(("parallel", "parallel"), 52),
        name="gated_mix",
    )(*ys, w_branch, gates, gates, gates)


def _layer_norm(y, g_ref, b_ref):
    mu = jnp.mean(y, axis=-1, keepdims=True)
    yc = y - mu
    var = jnp.mean(yc * yc, axis=-1, keepdims=True)
    return yc * lax.rsqrt(var + LN_EPS) * g_ref[...] + b_ref[...]


def _ln1_kernel(s_ref, g_ref, b_ref, ob_ref):
    ob_ref[...] = _layer_norm(s_ref[...], g_ref, b_ref).astype(BF16)


def _ln2_kernel(s1_ref, s2_ref, g1_ref, b1_ref, g2_ref, b2_ref, of_ref, *maybe_ob_ref, alpha):
    x1 = _layer_norm(s1_ref[...], g1_ref, b1_ref)
    out = _layer_norm(alpha * x1 + s2_ref[...], g2_ref, b2_ref)
    of_ref[...] = out
    for ob_ref in maybe_ob_ref:
        ob_ref[...] = out.astype(BF16)


LN_ROWS = 256


def layer_norm_bf16(s1, g1, b1):
    m, d = s1.shape
    row = pl.BlockSpec((LN_ROWS, d), lambda i: (i, 0))
    vec = pl.BlockSpec((1, d), lambda i: (0, 0))
    return pl.pallas_call(
        _ln1_kernel,
        grid=(m // LN_ROWS,),
        in_specs=[row, vec, vec],
        out_specs=row,
        out_shape=jax.ShapeDtypeStruct((m, d), BF16),
        compiler_params=_cparams(("parallel",), 40),
        name="ln1",
    )(s1, g1.reshape(1, d), b1.reshape(1, d))


def layer_norm_ffn_residual(s1, s2, g1, b1, g2, b2, alpha, with_bf16):
    m, d = s1.shape
    row = pl.BlockSpec((LN_ROWS, d), lambda i: (i, 0))
    vec = pl.BlockSpec((1, d), lambda i: (0, 0))
    out_shape = [jax.ShapeDtypeStruct((m, d), F32)] + [jax.ShapeDtypeStruct((m, d), BF16)] * with_bf16
    res = pl.pallas_call(
        functools.partial(_ln2_kernel, alpha=alpha),
        grid=(m // LN_ROWS,),
        in_specs=[row, row, vec, vec, vec, vec],
        out_specs=[row] * len(out_shape),
        out_shape=out_shape,
        compiler_params=_cparams(("parallel",), 40),
        name="ln2",
    )(s1, s2, g1.reshape(1, d), b1.reshape(1, d), g2.reshape(1, d), b2.reshape(1, d))
    return (res[0], res[1]) if with_bf16 else (res[0], None)


def kernel(x, w_in, b_gate, w_short_conv, na_rpb, gla_w_decay, gla_b_decay, gla_norm_g, w_branch, w_out,
           ln1_g, ln1_b, w_ffn_in, w_ffn_conv, b_ffn_conv, w_ffn_out, ln2_g, ln2_b):
    bsz, s, d = x.shape
    depth = w_in.shape[0]
    assert (s, d) == (SEQ, D_MODEL) and w_in.shape[2] == OFF_GATE + N_BRANCH * D_MODEL
    assert w_ffn_in.shape[2] == 2 * D_FF
    alpha = (2.0 * depth) ** 0.25
    ff_pad = D_FF_PAD - D_FF

    w_in_t = jnp.swapaxes(w_in, 1, 2)
    w_br_b = w_branch.astype(BF16)
    w_out_b = w_out.astype(BF16)
    w_conv_ff = jnp.pad(w_ffn_conv, ((0, 0), (0, 0), (0, ff_pad)))
    b_conv_ff = jnp.pad(b_ffn_conv, ((0, 0), (0, ff_pad)))

    xf = x.reshape(bsz * s, d)
    xb = xf.astype(BF16)
    for l in range(depth):
        hm = matmul_wres(xb, w_in_t, l, 0, MAIN_WIDTH, BF16, cast=ffn_in_padcast(w_ffn_in) if l == 0 else None)
        gates = matmul_wres(xb, w_in_t, l, OFF_GATE, N_BRANCH * D_MODEL, BF16, bias=b_gate[l],
                            cast=ffn_out_padcast(w_ffn_out) if l == 0 else None)
        if l == 0:
            (hm, w_ff_b), (gates, w_ffo_b) = hm, gates
        a_lr = matmul_wres(xb, w_in_t, l, MAIN_WIDTH, LANES, F32, tn=LANES)

        y_conv = short_conv_mixer(hm, w_short_conv[l], bsz)
        y_na = neighbourhood_attention(hm, _na_pair_tables(na_rpb[l]), bsz)
        wdec = jnp.zeros((2, LANES, GLA_KEY_WIDTH), F32)
        wdec = wdec.at[0, :GLA_GATE_RANK].set(gla_w_decay[l, 0])
        wdec = wdec.at[1, GLA_GATE_RANK:2 * GLA_GATE_RANK].set(gla_w_decay[l, 1])
        o_f, o_b = gla_scan(hm, a_lr, wdec.astype(BF16), gla_b_decay[l], bsz)
        y_gla = gla_output(o_f, o_b, hm, gla_norm_g[l])

        mixed = gated_mix((y_conv, y_na, y_gla), w_br_b, l, gates)
        s1 = matmul_residual(mixed, w_out_b, l, xf, alpha)
        xb = layer_norm_bf16(s1, ln1_g[l], ln1_b[l])

        g = ffn_in(xb, w_ff_b, l, w_conv_ff[l], b_conv_ff[l].reshape(1, D_FF_PAD), D_FF)
        s2 = matmul_kacc(g, w_ffo_b, l, D_FF)
        xf, xb = layer_norm_ffn_residual(s1, s2, ln1_g[l], ln1_b[l], ln2_g[l], ln2_b[l], alpha,
                                         with_bf16=l + 1 < depth)
    return xf.reshape(bsz, s, d)
```
